```python
import jax, jax.numpy as jnp
from jax import lax
import numpy as np

D_MODEL = 1024
BATCH = 4
SEQ = 4096
DEPTH = 1
DEC_BATCH = 8
DEC_SEQ = 16
PAST_LEN = 1024

CHUNK = 64
D_RNN = D_MODEL
N_LRU_BLOCKS = 16
LRU_BLOCK = D_RNN // N_LRU_BLOCKS
LRU_C = 8.0
RNN_CONV_W = 4
N_HEADS = 8
HEAD_DIM = D_MODEL // N_HEADS
D_ATTN = N_HEADS * HEAD_DIM
Q_BLOCK = 128
D_FF = 3 * D_MODEL
FFN_CONV_W = 3
NORM_EPS = 1e-6
D_IN = 2 * D_RNN + 3 * D_ATTN + 2 * D_MODEL

kernel_name = "hawk_stickbreak_convffn_stream"


def _rmsnorm(x, g):
    x32 = x.astype(jnp.float32)
    y = x32 * lax.rsqrt(jnp.mean(x32 * x32, axis=-1, keepdims=True) + NORM_EPS)
    return y.astype(x.dtype) * g


def _causal_dwconv(x, hist, w, b):
    W = w.shape[0]
    T = x.shape[1]
    xp = jnp.concatenate([hist.astype(x.dtype), x], axis=1)
    y = xp[:, 0:T] * w[0]
    for i in range(1, W):
        y = y + xp[:, i:i + T] * w[i]
    return y + b, xp[:, T:]


def _block_diag(x, w, b):
    B, T, C = x.shape
    xb = x.reshape(B, T, N_LRU_BLOCKS, LRU_BLOCK)
    return jnp.einsum('btnc,ncd->btnd', xb, w).reshape(B, T, C) + b


def _linear_scan(a, b, h0):
    b = b.at[:, 0].add(a[:, 0] * h0)
    def comb(l, r):
        return (l[0] * r[0], r[0] * l[1] + r[1])
    _, h = lax.associative_scan(comb, (a, b), axis=1)
    return h


def _rg_lru(x, h0, wa, ba, wx, bx, lam):
    r = jax.nn.sigmoid(_block_diag(x, wa, ba).astype(jnp.float32))
    i = jax.nn.sigmoid(_block_diag(x, wx, bx).astype(jnp.float32))
    log_a = -LRU_C * r * jax.nn.softplus(-lam.astype(jnp.float32))
    a = jnp.exp(log_a)
    b = jnp.sqrt(-jnp.expm1(2.0 * log_a)) * (i * x.astype(jnp.float32))
    h = _linear_scan(a, b, h0.astype(jnp.float32))
    return h.astype(x.dtype), h[:, -1].astype(x.dtype)


def _stick_breaking(q, k, v, q_pos, k_pos):
    z = jnp.einsum('bqhd,bkhd->bhqk', q, k).astype(jnp.float32) * (HEAD_DIM ** -0.5)
    mask = k_pos[None, :] < q_pos[:, None]
    log_beta = jax.nn.log_sigmoid(z)
    log_1m = jnp.where(mask, jax.nn.log_sigmoid(-z), 0.0)
    suffix = lax.cumsum(log_1m, axis=3, reverse=True) - log_1m
    w = jnp.where(mask, jnp.exp(log_beta + suffix), 0.0)
    return jnp.einsum('bhqk,bkhd->bqhd', w.astype(v.dtype), v)


def _stick_breaking_prompt(q, k, v):
    B, T, H, Dh = q.shape
    nb = T // Q_BLOCK
    qb = q.reshape(B, nb, Q_BLOCK, H, Dh).transpose(1, 0, 2, 3, 4)
    pos = jnp.arange(T, dtype=jnp.int32)
    qpos = pos.reshape(nb, Q_BLOCK)
    out = lax.map(lambda args: _stick_breaking(args[0], k, v, args[1], pos), (qb, qpos))
    return out.transpose(1, 0, 2, 3, 4).reshape(B, T, H, Dh)


def _layer(x, conv_hist, h0, k_past, v_past, ffn_hist,
           ln1, w_in, rnn_conv_w, rnn_conv_b, lru_wa, lru_ba, lru_wx, lru_bx, lru_lambda,
           q_norm_g, k_norm_g, w_proj_rnn, w_proj_attn, w_out,
           ln2, w_up, ffn_conv_w, ffn_conv_b, w_down):
    B, T, _ = x.shape
    u = _rmsnorm(x, ln1)
    proj = u @ w_in
    splits = [D_RNN, 2 * D_RNN, 2 * D_RNN + D_ATTN, 2 * D_RNN + 2 * D_ATTN,
              2 * D_RNN + 3 * D_ATTN, 2 * D_RNN + 3 * D_ATTN + D_MODEL]
    xr, gr, q, k, v, g_rnn, g_attn = jnp.split(proj, splits, axis=-1)
    xc, conv_tail = _causal_dwconv(xr, conv_hist, rnn_conv_w, rnn_conv_b)
    hseq, h_last = _rg_lru(xc, h0, lru_wa, lru_ba, lru_wx, lru_bx, lru_lambda)
    y_rnn = hseq * jax.nn.gelu(gr, approximate=True)
    q = _rmsnorm(q.reshape(B, T, N_HEADS, HEAD_DIM), q_norm_g)
    k = _rmsnorm(k.reshape(B, T, N_HEADS, HEAD_DIM), k_norm_g)
    v = v.reshape(B, T, N_HEADS, HEAD_DIM)
    if k_past is None:
        o = _stick_breaking_prompt(q, k, v)
    else:
        P = k_past.shape[1]
        k_all = jnp.concatenate([k_past.astype(k.dtype), k], axis=1)
        v_all = jnp.concatenate([v_past.astype(v.dtype), v], axis=1)
        q_pos = P + jnp.arange(T, dtype=jnp.int32)
        k_pos = jnp.arange(P + T, dtype=jnp.int32)
        o = _stick_breaking(q, k_all, v_all, q_pos, k_pos)
    y_attn = o.reshape(B, T, D_ATTN)
    merged = jax.nn.sigmoid(g_rnn) * (y_rnn @ w_proj_rnn) + jax.nn.sigmoid(g_attn) * (y_attn @ w_proj_attn)
    x = x + merged @ w_out
    u2 = _rmsnorm(x, ln2)
    gate_pre, val = jnp.split(u2 @ w_up, 2, axis=-1)
    gc, ffn_tail = _causal_dwconv(gate_pre, ffn_hist, ffn_conv_w, ffn_conv_b)
    x = x + (jax.nn.gelu(gc, approximate=True) * val) @ w_down
    return x, (k, v, conv_tail, h_last, ffn_tail)


def setup_inputs(seed: int = 0) -> dict:
    key = jax.random.key(seed)
    ks = iter(jax.random.split(key, 32))
    L = DEPTH
    def nrm(shape, scale=1.0):
        return jax.random.normal(next(ks), shape, jnp.float32) * scale
    u = jax.random.uniform(next(ks), (L, D_RNN), jnp.float32, minval=0.9, maxval=0.999)
    s = u ** (1.0 / LRU_C)
    return {
        "x_prompt": nrm((BATCH, SEQ, D_MODEL)),
        "x_sample": nrm((DEC_BATCH, DEC_SEQ, D_MODEL)),
        "cache_k": nrm((L, DEC_BATCH, PAST_LEN, N_HEADS, HEAD_DIM)),
        "cache_v": nrm((L, DEC_BATCH, PAST_LEN, N_HEADS, HEAD_DIM)),
        "state_rnn_conv": nrm((L, DEC_BATCH, RNN_CONV_W - 1, D_RNN)),
        "state_rnn_h": nrm((L, DEC_BATCH, D_RNN), 0.5),
        "state_ffn_conv": nrm((L, DEC_BATCH, FFN_CONV_W - 1, D_FF)),
        "ln1": 1.0 + nrm((L, D_MODEL), 0.01),
        "w_in": nrm((L, D_MODEL, D_IN), D_MODEL ** -0.5),
        "rnn_conv_w": nrm((L, RNN_CONV_W, D_RNN), RNN_CONV_W ** -0.5),
        "rnn_conv_b": nrm((L, D_RNN), 0.01),
        "lru_wa": nrm((L, N_LRU_BLOCKS, LRU_BLOCK, LRU_BLOCK), LRU_BLOCK ** -0.5),
        "lru_ba": nrm((L, D_RNN), 0.01),
        "lru_wx": nrm((L, N_LRU_BLOCKS, LRU_BLOCK, LRU_BLOCK), LRU_BLOCK ** -0.5),
        "lru_bx": nrm((L, D_RNN), 0.01),
        "lru_lambda": jnp.log(s) - jnp.log1p(-s),
        "q_norm_g": 1.0 + nrm((L, HEAD_DIM), 0.01),
        "k_norm_g": 1.0 + nrm((L, HEAD_DIM), 0.01),
        "w_proj_rnn": nrm((L, D_RNN, D_MODEL), D_RNN ** -0.5),
        "w_proj_attn": nrm((L, D_ATTN, D_MODEL), D_ATTN ** -0.5),
        "w_out": nrm((L, D_MODEL, D_MODEL), D_MODEL ** -0.5),
        "ln2": 1.0 + nrm((L, D_MODEL), 0.01),
        "w_up": nrm((L, D_MODEL, 2 * D_FF), D_MODEL ** -0.5),
        "ffn_conv_w": nrm((L, FFN_CONV_W, D_FF), FFN_CONV_W ** -0.5),
        "ffn_conv_b": nrm((L, D_FF), 0.01),
        "w_down": nrm((L, D_FF, D_MODEL), D_FF ** -0.5),
    }


def reference(x_prompt, x_sample, cache_k, cache_v, state_rnn_conv, state_rnn_h, state_ffn_conv,
              ln1, w_in, rnn_conv_w, rnn_conv_b, lru_wa, lru_ba, lru_wx, lru_bx, lru_lambda,
              q_norm_g, k_norm_g, w_proj_rnn, w_proj_attn, w_out,
              ln2, w_up, ffn_conv_w, ffn_conv_b, w_down):
    assert x_sample.shape[1] <= CHUNK
    yp, ys = x_prompt, x_sample
    Bp = x_prompt.shape[0]
    st_p, st_s = [], []
    for l in range(DEPTH):
        W = (ln1[l], w_in[l], rnn_conv_w[l], rnn_conv_b[l], lru_wa[l], lru_ba[l], lru_wx[l], lru_bx[l],
             lru_lambda[l], q_norm_g[l], k_norm_g[l], w_proj_rnn[l], w_proj_attn[l], w_out[l],
             ln2[l], w_up[l], ffn_conv_w[l], ffn_conv_b[l], w_down[l])
        zc = jnp.zeros((Bp, RNN_CONV_W - 1, D_RNN), yp.dtype)
        zh = jnp.zeros((Bp, D_RNN), yp.dtype)
        zf = jnp.zeros((Bp, FFN_CONV_W - 1, D_FF), yp.dtype)
        yp, sp = _layer(yp, zc, zh, None, None, zf, *W)
        ys, ss = _layer(ys, state_rnn_conv[l], state_rnn_h[l], cache_k[l], cache_v[l], state_ffn_conv[l], *W)
        st_p.append(sp)
        st_s.append(ss)
    k_p = jnp.stack([s[0] for s in st_p])
    v_p = jnp.stack([s[1] for s in st_p])
    rc_p = jnp.stack([s[2] for s in st_p])
    h_p = jnp.stack([s[3] for s in st_p])
    fc_p = jnp.stack([s[4] for s in st_p])
    k_s = jnp.stack([s[0] for s in st_s])
    v_s = jnp.stack([s[1] for s in st_s])
    rc_s = jnp.stack([s[2] for s in st_s])
    h_s = jnp.stack([s[3] for s in st_s])
    fc_s = jnp.stack([s[4] for s in st_s])
    return (yp, ys, k_p, v_p, rc_p, h_p, fc_p, k_s, v_s, rc_s, h_s, fc_s)
```

```python
import functools
import math

import jax
import jax.numpy as jnp
from jax import lax
from jax.experimental import pallas as pl
from jax.experimental.pallas import tpu as pltpu

F32 = jnp.float32
BF16 = jnp.bfloat16

NORM_EPS = 1e-6
LRU_C = 8.0
N_HEADS = 8
HEAD_DIM = 128
N_LRU_BLOCKS = 16
LANE = 128
GATE_GROUP = 256
VMEM_LIMIT_BYTES = 56 * 1024 * 1024
SUFFIX_FLOOR = -104.0


def _gelu_tanh(x):
    return 0.5 * x * (1.0 + jnp.tanh(math.sqrt(2.0 / math.pi) * (x + 0.044715 * (x * x * x))))


def _softplus(x):
    return jnp.maximum(x, 0.0) + jnp.log1p(jnp.exp(-jnp.abs(x)))


def _rms_scale(x):
    return lax.rsqrt(jnp.mean(x * x, axis=-1, keepdims=True) + NORM_EPS)


def _head_rmsnorm(x, g):
    outs = []
    for h in range(N_HEADS):
        xh = x[:, h * HEAD_DIM:(h + 1) * HEAD_DIM]
        outs.append(xh * _rms_scale(xh) * g)
    return outs


def _front_kernel(x_ref, hist_ref, h0_ref, ln1_ref, w_in_ref, cw_ref, cb_ref, wg_ref, ba_ref, bx_ref,
                  lam_ref, qg_ref, kg_ref, wpr_ref,
                  q_ref, k_ref, v_ref, m_ref, sg_ref, tail_ref, hl_ref,
                  u_ref, xp_ref, a_ref, b_ref, hc_ref, *, bb, tm, d):
    i = pl.program_id(1)
    rows = bb * tm
    kw = cw_ref.shape[0]
    pad = 8

    @pl.when(i == 0)
    def _():
        xp_ref[:, pad - (kw - 1):pad, :] = hist_ref[...]
        hc_ref[...] = h0_ref[...]

    x = x_ref[...].reshape(rows, d)
    u_ref[...] = ((x * _rms_scale(x)) * ln1_ref[...]).astype(BF16)

    def proj(c):
        return jnp.dot(u_ref[...], w_in_ref[:, c * d:(c + 1) * d], preferred_element_type=F32)

    xr = proj(0)
    for b in range(bb):
        xp_ref[b, pad:pad + tm, :] = xr[b * tm:(b + 1) * tm]
    xcs = []
    for b in range(bb):
        off = pad - (kw - 1)
        y = xp_ref[b, off:off + tm, :] * cw_ref[0:1, :]
        for j in range(1, kw):
            y = y + xp_ref[b, off + j:off + j + tm, :] * cw_ref[j:j + 1, :]
        xcs.append(y + cb_ref[...])
        tail = xp_ref[b, pad + tm - (kw - 1):pad + tm, :]
        tail_ref[b] = tail
        xp_ref[b, pad - (kw - 1):pad, :] = tail
    xc = xcs[0] if bb == 1 else jnp.concatenate(xcs, axis=0)

    sp_lam = _softplus(-lam_ref[...])
    for g in range(d // GATE_GROUP):
        sl = slice(g * GATE_GROUP, (g + 1) * GATE_GROUP)
        xg = xc[:, sl]
        pre = jnp.dot(xg.astype(BF16), wg_ref[g], preferred_element_type=F32)
        r = jax.nn.sigmoid(pre[:, :GATE_GROUP] + ba_ref[:, sl])
        ig = jax.nn.sigmoid(pre[:, GATE_GROUP:] + bx_ref[:, sl])
        log_a = (-LRU_C) * r * sp_lam[:, sl]
        a = jnp.exp(log_a)
        one_m_a2 = -jnp.tanh(log_a) * (a * a + 1.0)
        a_ref[:, sl] = a
        b_ref[:, sl] = jnp.sqrt(one_m_a2) * (ig * xg)

    for b in range(bb):
        def step(t, h, b=b):
            r = b * tm + t
            h = a_ref[pl.ds(r, 1), :] * h + b_ref[pl.ds(r, 1), :]
            b_ref[pl.ds(r, 1), :] = h
            return h
        h_end = lax.fori_loop(0, tm, step, hc_ref[b], unroll=8)
        hc_ref[b] = h_end
        hl_ref[b] = h_end

    y_rnn = (b_ref[...] * _gelu_tanh(proj(1))).astype(BF16)
    m = jax.nn.sigmoid(proj(5)) * jnp.dot(y_rnn, wpr_ref[...], preferred_element_type=F32)
    m_ref[...] = m.astype(BF16).reshape(bb, tm, d)

    qn = _head_rmsnorm(proj(2), qg_ref[...])
    for h in range(N_HEADS):
        q_ref[:, :, h * HEAD_DIM:(h + 1) * HEAD_DIM] = qn[h].astype(BF16).reshape(bb, tm, HEAD_DIM)
    kn = _head_rmsnorm(proj(3), kg_ref[...])
    for h in range(N_HEADS):
        k_ref[:, :, h * HEAD_DIM:(h + 1) * HEAD_DIM] = kn[h].reshape(bb, tm, HEAD_DIM)
    v_ref[...] = proj(4).reshape(bb, tm, d)
    sg_ref[...] = jax.nn.sigmoid(proj(6)).astype(BF16).reshape(bb, tm, d)


def _front(x, hist, h0, ln1, w_in, cw, cb, wg, ba, bx, lam, qg, kg, wpr, *, bb, tm):
    nb, t, d = x.shape
    kw = cw.shape[0]
    grid = (nb // bb, t // tm)
    tile = lambda: pl.BlockSpec((bb, tm, d), lambda b, i: (b, i, 0))
    full = lambda a: pl.BlockSpec(a.shape, lambda b, i: (0,) * a.ndim)
    per_b = lambda n: pl.BlockSpec((bb, n, d), lambda b, i: (b, 0, 0))
    act = lambda dt: jax.ShapeDtypeStruct((nb, t, d), dt)
    return pl.pallas_call(
        functools.partial(_front_kernel, bb=bb, tm=tm, d=d),
        grid=grid,
        in_specs=[tile(), per_b(kw - 1), per_b(1)] + [full(a) for a in (ln1, w_in, cw, cb, wg, ba, bx, lam, qg, kg, wpr)],
        out_specs=[tile(), tile(), tile(), tile(), tile(), per_b(kw - 1), per_b(1)],
        out_shape=[act(BF16), act(F32), act(F32), act(BF16), act(BF16),
                   jax.ShapeDtypeStruct((nb, kw - 1, d), F32), jax.ShapeDtypeStruct((nb, 1, d), F32)],
        scratch_shapes=[pltpu.VMEM((bb * tm, d), BF16), pltpu.VMEM((bb, 8 + tm, d), F32),
                        pltpu.VMEM((bb * tm, d), F32), pltpu.VMEM((bb * tm, d), F32),
                        pltpu.VMEM((bb, 1, d), F32)],
        compiler_params=pltpu.CompilerParams(dimension_semantics=("arbitrary", "arbitrary"),
                                             vmem_limit_bytes=VMEM_LIMIT_BYTES),
        name="front",
    )(x, hist, h0, ln1, w_in, cw, cb, wg, ba, bx, lam, qg, kg, wpr)


def _attn_kernel(*refs, bq, bkn, bkc, heads, has_cache):
    if has_cache:
        q_ref, kn_ref, vn_ref, kc_ref, vc_ref, o_ref, acc_ref, r_ref = refs
    else:
        q_ref, kn_ref, vn_ref, o_ref, acc_ref, r_ref = refs
        kc_ref = vc_ref = None
    i = pl.program_id(2)
    scale = HEAD_DIM ** -0.5
    acc_ref[...] = jnp.zeros_like(acc_ref)
    r_ref[...] = jnp.zeros_like(r_ref)

    def suffix_matrix(bk):
        rr = lax.broadcasted_iota(jnp.int32, (bk, bk), 0)
        cc = lax.broadcasted_iota(jnp.int32, (bk, bk), 1)
        return (rr > cc).astype(BF16)

    def block(k_ref, v_ref, j, bk, masked):
        u = suffix_matrix(bk)
        if masked:
            t_loc = i * bq + lax.broadcasted_iota(jnp.int32, (bq, bk), 0)
            s_loc = j * bk + lax.broadcasted_iota(jnp.int32, (bq, bk), 1)
            valid = s_loc < t_loc
        for h in range(heads):
            sl = slice(h * HEAD_DIM, (h + 1) * HEAD_DIM)
            kj = k_ref[0, pl.ds(j * bk, bk), sl].astype(BF16)
            vj = v_ref[0, pl.ds(j * bk, bk), sl].astype(BF16)
            z = lax.dot_general(q_ref[0, :, sl], kj, (((1,), (1,)), ((), ())),
                                preferred_element_type=F32) * scale
            sp = jnp.log1p(jnp.exp(-jnp.abs(z)))
            log_beta = jnp.minimum(z, 0.0) - sp
            log_1m = log_beta - z
            if masked:
                log_1m = jnp.where(valid, log_1m, 0.0)
            hi = log_1m.astype(BF16)
            lo = (log_1m - hi.astype(F32)).astype(BF16)
            suffix = (jnp.dot(hi, u, preferred_element_type=F32) + jnp.dot(lo, u, preferred_element_type=F32))
            w = jnp.exp(log_beta + suffix + r_ref[h])
            if masked:
                w = jnp.where(valid, w, 0.0)
            acc_ref[h] += jnp.dot(w.astype(BF16), vj, preferred_element_type=F32)
            r_ref[h] += jnp.sum(log_1m, axis=-1, keepdims=True)

    def live():
        return jnp.max(r_ref[...]) > SUFFIX_FLOOR

    def walk(k_ref, v_ref, j_start, alive, bk, masked):
        def cond(c):
            return jnp.logical_and(c[0] >= 0, c[1])

        def body(c):
            block(k_ref, v_ref, c[0], bk, masked)
            return c[0] - 1, live()
        lax.while_loop(cond, body, (j_start, alive))

    walk(kn_ref, vn_ref, pl.cdiv((i + 1) * bq, bkn) - 1, jnp.bool_(True), bkn, True)
    if has_cache:
        walk(kc_ref, vc_ref, jnp.int32(kc_ref.shape[1] // bkc - 1), live(), bkc, False)
    for h in range(heads):
        o_ref[0, :, h * HEAD_DIM:(h + 1) * HEAD_DIM] = acc_ref[h].astype(o_ref.dtype)


def _attention(q, kn, vn, kc=None, vc=None, *, bq, bkn, bkc=128, heads):
    nb, t, d = q.shape
    hw = heads * HEAD_DIM
    has_cache = kc is not None
    grid = (nb, d // hw, t // bq)
    qspec = pl.BlockSpec((1, bq, hw), lambda b, g, i: (b, i, g))
    kvspec = lambda a: pl.BlockSpec((1, a.shape[1], hw), lambda b, g, i: (b, 0, g))
    ins = [q, kn, vn] + ([kc, vc] if has_cache else [])
    return pl.pallas_call(
        functools.partial(_attn_kernel, bq=bq, bkn=bkn, bkc=bkc, heads=heads, has_cache=has_cache),
        grid=grid,
        in_specs=[qspec] + [kvspec(a) for a in ins[1:]],
        out_specs=qspec,
        out_shape=jax.ShapeDtypeStruct((nb, t, d), BF16),
        scratch_shapes=[pltpu.VMEM((heads, bq, HEAD_DIM), F32), pltpu.VMEM((heads, bq, 1), F32)],
        compiler_params=pltpu.CompilerParams(dimension_semantics=("arbitrary", "arbitrary", "arbitrary"),
                                             vmem_limit_bytes=VMEM_LIMIT_BYTES),
        name="attn_cache" if has_cache else "attn",
    )(*ins)


def _back_kernel(x_ref, m_ref, sg_ref, ya_ref, fh_ref, wpa_ref, wo_ref, ln2_ref, wup_ref, fw_ref, fb_ref, wdn_ref,
                 y_ref, ft_ref, gp_ref, u2_ref, x1_ref, acc_ref, *, bb, tm, d, dff):
    i = pl.program_id(1)
    rows = bb * tm
    kw = fw_ref.shape[0]
    pad = 8
    nchunk = dff // d

    @pl.when(i == 0)
    def _():
        for c in range(nchunk):
            gp_ref[c, :, pad - (kw - 1):pad, :] = fh_ref[:, :, c * d:(c + 1) * d]

    x = x_ref[...].reshape(rows, d)
    att = jnp.dot(ya_ref[...].reshape(rows, d), wpa_ref[...], preferred_element_type=F32)
    merged = m_ref[...].reshape(rows, d).astype(F32) + sg_ref[...].reshape(rows, d).astype(F32) * att
    x1 = x + jnp.dot(merged.astype(BF16), wo_ref[...], preferred_element_type=F32)
    x1_ref[...] = x1
    u2_ref[...] = ((x1 * _rms_scale(x1)) * ln2_ref[...]).astype(BF16)

    for c in range(nchunk):
        cs = slice(c * d, (c + 1) * d)
        gp = jnp.dot(u2_ref[...], wup_ref[:, c * d:(c + 1) * d], preferred_element_type=F32)
        for b in range(bb):
            gp_ref[c, b, pad:pad + tm, :] = gp[b * tm:(b + 1) * tm]
        gcs = []
        for b in range(bb):
            off = pad - (kw - 1)
            g = gp_ref[c, b, off:off + tm, :] * fw_ref[0:1, cs]
            for j in range(1, kw):
                g = g + gp_ref[c, b, off + j:off + j + tm, :] * fw_ref[j:j + 1, cs]
            gcs.append(g + fb_ref[:, cs])
            tail = gp_ref[c, b, pad + tm - (kw - 1):pad + tm, :]
            ft_ref[b, :, cs] = tail
            gp_ref[c, b, pad - (kw - 1):pad, :] = tail
        gc = gcs[0] if bb == 1 else jnp.concatenate(gcs, axis=0)
        val = jnp.dot(u2_ref[...], wup_ref[:, dff + c * d:dff + (c + 1) * d], preferred_element_type=F32)
        act = (_gelu_tanh(gc) * val).astype(BF16)
        part = jnp.dot(act, wdn_ref[c * d:(c + 1) * d, :], preferred_element_type=F32)
        if c == 0:
            acc_ref[...] = part
        else:
            acc_ref[...] += part
    y_ref[...] = (x1_ref[...] + acc_ref[...]).reshape(bb, tm, d)


def _back(x, m, sg, ya, fh, wpa, wo, ln2, wup, fw, fb, wdn, *, bb, tm):
    nb, t, d = x.shape
    dff = wdn.shape[0]
    kw = fw.shape[0]
    grid = (nb // bb, t // tm)
    tile = lambda: pl.BlockSpec((bb, tm, d), lambda b, i: (b, i, 0))
    full = lambda a: pl.BlockSpec(a.shape, lambda b, i: (0,) * a.ndim)
    hist = pl.BlockSpec((bb, kw - 1, dff), lambda b, i: (b, 0, 0))
    return pl.pallas_call(
        functools.partial(_back_kernel, bb=bb, tm=tm, d=d, dff=dff),
        grid=grid,
        in_specs=[tile(), tile(), tile(), tile(), hist] + [full(a) for a in (wpa, wo, ln2, wup, fw, fb, wdn)],
        out_specs=[tile(), hist],
        out_shape=[jax.ShapeDtypeStruct((nb, t, d), F32), jax.ShapeDtypeStruct((nb, kw - 1, dff), F32)],
        scratch_shapes=[pltpu.VMEM((dff // d, bb, 8 + tm, d), F32), pltpu.VMEM((bb * tm, d), BF16),
                        pltpu.VMEM((bb * tm, d), F32), pltpu.VMEM((bb * tm, d), F32)],
        compiler_params=pltpu.CompilerParams(dimension_semantics=("arbitrary", "arbitrary"),
                                             vmem_limit_bytes=VMEM_LIMIT_BYTES),
        name="back",
    )(x, m, sg, ya, fh, wpa, wo, ln2, wup, fw, fb, wdn)


def _dense_gate_blocks(w):
    nblk, c, _ = w.shape
    per = GATE_GROUP // c
    w4 = w.reshape(nblk // per, per, c, c)
    eye = jnp.eye(per, dtype=w.dtype)
    return jnp.einsum('gncd,nm->gncmd', w4, eye).reshape(nblk // per, GATE_GROUP, GATE_GROUP)


def _layer(x, hist, h0, fh, kc, vc, wts, *, bb, tm, bq, bkn, heads):
    (ln1, w_in, cw, cb, wg, ba, bx, lam, qg, kg, wpr, wpa, wo, ln2, wup, fw, fb, wdn) = wts
    q, k, v, m, sg, tail, hl = _front(x, hist, h0, ln1, w_in, cw, cb, wg, ba, bx, lam, qg, kg, wpr, bb=bb, tm=tm)
    t = k.shape[1]
    tpad = -(-t // bkn) * bkn
    kn, vn = (k, v) if tpad == t else tuple(jnp.pad(a, ((0, 0), (0, tpad - t), (0, 0))) for a in (k, v))
    ya = _attention(q, kn, vn, kc, vc, bq=bq, bkn=bkn, heads=heads)
    y, ft = _back(x, m, sg, ya, fh, wpa, wo, ln2, wup, fw, fb, wdn, bb=bb, tm=tm)
    return y, k, v, tail, hl, ft


def kernel(x_prompt, x_sample, cache_k, cache_v, state_rnn_conv, state_rnn_h, state_ffn_conv, ln1, w_in, rnn_conv_w, rnn_conv_b, lru_wa, lru_ba, lru_wx, lru_bx, lru_lambda, q_norm_g, k_norm_g, w_proj_rnn, w_proj_attn, w_out, ln2, w_up, ffn_conv_w, ffn_conv_b, w_down):
    depth = ln1.shape[0]
    assert depth == 1
    bp, tp, d = x_prompt.shape
    bs, ts, _ = x_sample.shape
    dff = w_down.shape[1]
    row = lambda a: a[0].reshape(1, -1)
    wg = jnp.concatenate([_dense_gate_blocks(lru_wa[0]), _dense_gate_blocks(lru_wx[0])], axis=-1).astype(BF16)
    wts = (row(ln1), w_in[0].astype(BF16), rnn_conv_w[0], row(rnn_conv_b), wg, row(lru_ba), row(lru_bx),
           row(lru_lambda), row(q_norm_g), row(k_norm_g), w_proj_rnn[0].astype(BF16),
           w_proj_attn[0].astype(BF16), w_out[0].astype(BF16), row(ln2), w_up[0].astype(BF16),
           ffn_conv_w[0], row(ffn_conv_b), w_down[0].astype(BF16))
    kw = rnn_conv_w.shape[1]
    fkw = ffn_conv_w.shape[1]

    zc = jnp.zeros((bp, kw - 1, d), F32)
    zh = jnp.zeros((bp, 1, d), F32)
    zf = jnp.zeros((bp, fkw - 1, w_down.shape[1]), F32)
    yp, kp, vp, rcp, hp, fcp = _layer(x_prompt, zc, zh, zf, None, None, wts,
                                      bb=1, tm=256, bq=256, bkn=128, heads=1)
    ys, ks, vs, rcs, hs, fcs = _layer(x_sample, state_rnn_conv[0], state_rnn_h[0].reshape(bs, 1, d),
                                      state_ffn_conv[0], cache_k[0].reshape(bs, -1, d),
                                      cache_v[0].reshape(bs, -1, d), wts,
                                      bb=bs, tm=ts, bq=ts, bkn=128, heads=N_HEADS)
    hd = lambda a: a.reshape(1, a.shape[0], a.shape[1], N_HEADS, HEAD_DIM)
    return (yp, ys, hd(kp), hd(vp), rcp[None], hp.reshape(1, bp, d), fcp[None],
            hd(ks), hd(vs), rcs[None], hs.reshape(1, bs, d), fcs[None])
```

```python
import functools
import math

import jax
import jax.numpy as jnp
from jax import lax
from jax.experimental import pallas as pl
from jax.experimental.pallas import tpu as pltpu

F32 = jnp.float32
BF16 = jnp.bfloat16

NORM_EPS = 1e-6
LRU_C = 8.0
N_HEADS = 8
HEAD_DIM = 128
N_LRU_BLOCKS = 16
LANE = 128
GATE_GROUP = 256
VMEM_LIMIT_BYTES = 56 * 1024 * 1024
SUFFIX_FLOOR = -104.0


def _gelu_tanh(x):
    return 0.5 * x * (1.0 + jnp.tanh(math.sqrt(2.0 / math.pi) * (x + 0.044715 * (x * x * x))))


def _softplus(x):
    return jnp.maximum(x, 0.0) + jnp.log1p(jnp.exp(-jnp.abs(x)))


def _rms_scale(x):
    return lax.rsqrt(jnp.mean(x * x, axis=-1, keepdims=True) + NORM_EPS)


def _head_rmsnorm(x, g):
    outs = []
    for h in range(N_HEADS):
        xh = x[:, h * HEAD_DIM:(h + 1) * HEAD_DIM]
        outs.append(xh * _rms_scale(xh) * g)
    return outs


def _front_kernel(x_ref, hist_ref, h0_ref, ln1_ref, w_in_ref, cw_ref, cb_ref, wg_ref, ba_ref, bx_ref,
                  lam_ref, qg_ref, kg_ref, wpr_ref,
                  q_ref, k_ref, v_ref, m_ref, sg_ref, tail_ref, hl_ref,
                  u_ref, xp_ref, a_ref, b_ref, hc_ref, *, bb, tm, d):
    i = pl.program_id(1)
    rows = bb * tm
    kw = cw_ref.shape[0]
    pad = 8

    @pl.when(i == 0)
    def _():
        xp_ref[:, pad - (kw - 1):pad, :] = hist_ref[...]
        hc_ref[...] = h0_ref[...]

    x = x_ref[...].reshape(rows, d)
    u_ref[...] = ((x * _rms_scale(x)) * ln1_ref[...]).astype(BF16)

    def proj(c):
        return jnp.dot(u_ref[...], w_in_ref[:, c * d:(c + 1) * d], preferred_element_type=F32)

    xr = proj(0)
    for b in range(bb):
        xp_ref[b, pad:pad + tm, :] = xr[b * tm:(b + 1) * tm]
    xcs = []
    for b in range(bb):
        off = pad - (kw - 1)
        y = xp_ref[b, off:off + tm, :] * cw_ref[0:1, :]
        for j in range(1, kw):
            y = y + xp_ref[b, off + j:off + j + tm, :] * cw_ref[j:j + 1, :]
        xcs.append(y + cb_ref[...])
        tail = xp_ref[b, pad + tm - (kw - 1):pad + tm, :]
        tail_ref[b] = tail
        xp_ref[b, pad - (kw - 1):pad, :] = tail
    xc = xcs[0] if bb == 1 else jnp.concatenate(xcs, axis=0)

    sp_lam = _softplus(-lam_ref[...])
    for g in range(d // GATE_GROUP):
        sl = slice(g * GATE_GROUP, (g + 1) * GATE_GROUP)
        xg = xc[:, sl]
        pre = jnp.dot(xg.astype(BF16), wg_ref[g], preferred_element_type=F32)
        r = jax.nn.sigmoid(pre[:, :GATE_GROUP] + ba_ref[:, sl])
        ig = jax.nn.sigmoid(pre[:, GATE_GROUP:] + bx_ref[:, sl])
        log_a = (-LRU_C) * r * sp_lam[:, sl]
        a = jnp.exp(log_a)
        one_m_a2 = -jnp.tanh(log_a) * (a * a + 1.0)
        a_ref[:, sl] = a
        b_ref[:, sl] = jnp.sqrt(one_m_a2) * (ig * xg)

    for b in range(bb):
        def step(t, h, b=b):
            r = b * tm + t
            h = a_ref[pl.ds(r, 1), :] * h + b_ref[pl.ds(r, 1), :]
            b_ref[pl.ds(r, 1), :] = h
            return h
        h_end = lax.fori_loop(0, tm, step, hc_ref[b], unroll=8)
        hc_ref[b] = h_end
        hl_ref[b] = h_end

    y_rnn = (b_ref[...] * _gelu_tanh(proj(1))).astype(BF16)
    m = jax.nn.sigmoid(proj(5)) * jnp.dot(y_rnn, wpr_ref[...], preferred_element_type=F32)
    m_ref[...] = m.astype(BF16).reshape(bb, tm, d)

    qn = _head_rmsnorm(proj(2), qg_ref[...])
    for h in range(N_HEADS):
        q_ref[:, :, h * HEAD_DIM:(h + 1) * HEAD_DIM] = qn[h].astype(BF16).reshape(bb, tm, HEAD_DIM)
    kn = _head_rmsnorm(proj(3), kg_ref[...])
    for h in range(N_HEADS):
        k_ref[:, :, h * HEAD_DIM:(h + 1) * HEAD_DIM] = kn[h].reshape(bb, tm, HEAD_DIM)
    v_ref[...] = proj(4).reshape(bb, tm, d)
    sg_ref[...] = jax.nn.sigmoid(proj(6)).astype(BF16).reshape(bb, tm, d)


def _front(x, hist, h0, ln1, w_in, cw, cb, wg, ba, bx, lam, qg, kg, wpr, *, bb, tm):
    nb, t, d = x.shape
    kw = cw.shape[0]
    grid = (nb // bb, t // tm)
    tile = lambda: pl.BlockSpec((bb, tm, d), lambda b, i: (b, i, 0))
    full = lambda a: pl.BlockSpec(a.shape, lambda b, i: (0,) * a.ndim)
    per_b = lambda n: pl.BlockSpec((bb, n, d), lambda b, i: (b, 0, 0))
    act = lambda dt: jax.ShapeDtypeStruct((nb, t, d), dt)
    return pl.pallas_call(
        functools.partial(_front_kernel, bb=bb, tm=tm, d=d),
        grid=grid,
        in_specs=[tile(), per_b(kw - 1), per_b(1)] + [full(a) for a in (ln1, w_in, cw, cb, wg, ba, bx, lam, qg, kg, wpr)],
        out_specs=[tile(), tile(), tile(), tile(), tile(), per_b(kw - 1), per_b(1)],
        out_shape=[act(BF16), act(F32), act(F32), act(BF16), act(BF16),
                   jax.ShapeDtypeStruct((nb, kw - 1, d), F32), jax.ShapeDtypeStruct((nb, 1, d), F32)],
        scratch_shapes=[pltpu.VMEM((bb * tm, d), BF16), pltpu.VMEM((bb, 8 + tm, d), F32),
                        pltpu.VMEM((bb * tm, d), F32), pltpu.VMEM((bb * tm, d), F32),
                        pltpu.VMEM((bb, 1, d), F32)],
        compiler_params=pltpu.CompilerParams(dimension_semantics=("arbitrary", "arbitrary"),
                                             vmem_limit_bytes=VMEM_LIMIT_BYTES),
        name="front",
    )(x, hist, h0, ln1, w_in, cw, cb, wg, ba, bx, lam, qg, kg, wpr)


def _attn_kernel(*refs, bq, bkd, bkp, heads, has_cache):
    if has_cache:
        q_ref, kn_ref, vn_ref, kp_ref, vp_ref, ud_ref, up_ref, o_ref, acc_ref, r_ref = refs
    else:
        q_ref, kn_ref, vn_ref, ud_ref, up_ref, o_ref, acc_ref, r_ref = refs
        kp_ref, vp_ref = kn_ref, vn_ref
    i = pl.program_id(2)
    scale = HEAD_DIM ** -0.5

    def block(k_ref, v_ref, u_ref, start, bk, valid, first):
        hs = range(heads)
        sls = [slice(h * HEAD_DIM, (h + 1) * HEAD_DIM) for h in hs]
        zs = [lax.dot_general(q_ref[0, :, sls[h]], k_ref[0, pl.ds(start, bk), sls[h]].astype(BF16),
                              (((1,), (1,)), ((), ())), preferred_element_type=F32) * scale for h in hs]
        log_betas, log_1ms, splits = [], [], []
        for h in hs:
            z = zs[h]
            log_beta = jnp.minimum(z, 0.0) - jnp.log(1.0 + jnp.exp(-jnp.abs(z)))
            log_1m = log_beta - z
            if valid is not None:
                log_1m = jnp.where(valid, log_1m, 0.0)
            hi = log_1m.astype(BF16)
            lo = (log_1m - hi.astype(F32)).astype(BF16)
            log_betas.append(log_beta)
            log_1ms.append(log_1m)
            splits.append(jnp.concatenate([hi, lo], axis=1))
        suffixes = [jnp.dot(splits[h], u_ref[...], preferred_element_type=F32) for h in hs]
        ws, rs_old = [], []
        for h in hs:
            arg = log_betas[h] + suffixes[h]
            if not first:
                r = r_ref[h]
                rs_old.append(r)
                arg = arg + (r if bk == LANE else jnp.concatenate([r] * (bk // LANE), axis=1))
            w = jnp.exp(arg)
            if valid is not None:
                w = jnp.where(valid, w, 0.0)
            ws.append(w.astype(BF16))
        pvs = [jnp.dot(ws[h], v_ref[0, pl.ds(start, bk), sls[h]].astype(BF16), preferred_element_type=F32)
               for h in hs]
        for h in hs:
            rs = jnp.broadcast_to(jnp.sum(log_1ms[h], axis=-1, keepdims=True), (bq, LANE))
            if first:
                acc_ref[h] = pvs[h]
                r_ref[h] = rs
            else:
                acc_ref[h] += pvs[h]
                r_ref[h] = rs_old[h] + rs

    def live():
        return jnp.max(r_ref[...]) > SUFFIX_FLOOR

    row = lax.broadcasted_iota(jnp.int32, (bq, bkd), 0)
    col = lax.broadcasted_iota(jnp.int32, (bq, bkd), 1)
    block(kn_ref, vn_ref, ud_ref, pl.multiple_of(i * bq, bq), bkd, col < row, True)

    j0 = jnp.int32(kp_ref.shape[1] // bkp - 1) if has_cache else i - 1

    def cond(c):
        return jnp.logical_and(c[0] >= 0, c[1])

    def body(c):
        block(kp_ref, vp_ref, up_ref, pl.multiple_of(c[0] * bkp, bkp), bkp, None, False)
        return c[0] - 1, live()
    lax.while_loop(cond, body, (j0, live()))

    for h in range(heads):
        o_ref[0, :, h * HEAD_DIM:(h + 1) * HEAD_DIM] = acc_ref[h].astype(o_ref.dtype)


def _suffix_matrix(bk):
    u = jnp.tril(jnp.ones((bk, bk), F32), -1).astype(BF16)
    return jnp.concatenate([u, u], axis=0)


def _attention(q, kn, vn, kc=None, vc=None, *, bq, bkd, bkp, heads):
    nb, t, d = q.shape
    hw = heads * HEAD_DIM
    has_cache = kc is not None
    assert has_cache or bkd == bkp == bq
    grid = (nb, d // hw, t // bq)
    qspec = pl.BlockSpec((1, bq, hw), lambda b, g, i: (b, i, g))
    kvspec = lambda a: pl.BlockSpec((1, a.shape[1], hw), lambda b, g, i: (b, 0, g))
    full = lambda a: pl.BlockSpec(a.shape, lambda b, g, i: (0,) * a.ndim)
    kv = [kn, vn] + ([kc, vc] if has_cache else [])
    us = [_suffix_matrix(bkd), _suffix_matrix(bkp)]
    return pl.pallas_call(
        functools.partial(_attn_kernel, bq=bq, bkd=bkd, bkp=bkp, heads=heads, has_cache=has_cache),
        grid=grid,
        in_specs=[qspec] + [kvspec(a) for a in kv] + [full(a) for a in us],
        out_specs=qspec,
        out_shape=jax.ShapeDtypeStruct((nb, t, d), BF16),
        scratch_shapes=[pltpu.VMEM((heads, bq, HEAD_DIM), F32), pltpu.VMEM((heads, bq, LANE), F32)],
        compiler_params=pltpu.CompilerParams(dimension_semantics=("arbitrary", "arbitrary", "arbitrary"),
                                             vmem_limit_bytes=VMEM_LIMIT_BYTES),
        name="attn_cache" if has_cache else "attn",
    )(q, *kv, *us)


def _back_kernel(x_ref, m_ref, sg_ref, ya_ref, fh_ref, wpa_ref, wo_ref, ln2_ref, wup_ref, fw_ref, fb_ref, wdn_ref,
                 y_ref, ft_ref, gp_ref, u2_ref, x1_ref, acc_ref, *, bb, tm, d, dff):
    i = pl.program_id(1)
    rows = bb * tm
    kw = fw_ref.shape[0]
    pad = 8
    nchunk = dff // d

    @pl.when(i == 0)
    def _():
        for c in range(nchunk):
            gp_ref[c, :, pad - (kw - 1):pad, :] = fh_ref[:, :, c * d:(c + 1) * d]

    x = x_ref[...].reshape(rows, d)
    att = jnp.dot(ya_ref[...].reshape(rows, d), wpa_ref[...], preferred_element_type=F32)
    merged = m_ref[...].reshape(rows, d).astype(F32) + sg_ref[...].reshape(rows, d).astype(F32) * att
    x1 = x + jnp.dot(merged.astype(BF16), wo_ref[...], preferred_element_type=F32)
    x1_ref[...] = x1
    u2_ref[...] = ((x1 * _rms_scale(x1)) * ln2_ref[...]).astype(BF16)

    for c in range(nchunk):
        cs = slice(c * d, (c + 1) * d)
        gp = jnp.dot(u2_ref[...], wup_ref[:, c * d:(c + 1) * d], preferred_element_type=F32)
        for b in range(bb):
            gp_ref[c, b, pad:pad + tm, :] = gp[b * tm:(b + 1) * tm]
        gcs = []
        for b in range(bb):
            off = pad - (kw - 1)
            g = gp_ref[c, b, off:off + tm, :] * fw_ref[0:1, cs]
            for j in range(1, kw):
                g = g + gp_ref[c, b, off + j:off + j + tm, :] * fw_ref[j:j + 1, cs]
            gcs.append(g + fb_ref[:, cs])
            tail = gp_ref[c, b, pad + tm - (kw - 1):pad + tm, :]
            ft_ref[b, :, cs] = tail
            gp_ref[c, b, pad - (kw - 1):pad, :] = tail
        gc = gcs[0] if bb == 1 else jnp.concatenate(gcs, axis=0)
        val = jnp.dot(u2_ref[...], wup_ref[:, dff + c * d:dff + (c + 1) * d], preferred_element_type=F32)
        act = (_gelu_tanh(gc) * val).astype(BF16)
        part = jnp.dot(act, wdn_ref[c * d:(c + 1) * d, :], preferred_element_type=F32)
        if c == 0:
            acc_ref[...] = part
        else:
            acc_ref[...] += part
    y_ref[...] = (x1_ref[...] + acc_ref[...]).reshape(bb, tm, d)


def _back(x, m, sg, ya, fh, wpa, wo, ln2, wup, fw, fb, wdn, *, bb, tm):
    nb, t, d = x.shape
    dff = wdn.shape[0]
    kw = fw.shape[0]
    grid = (nb // bb, t // tm)
    tile = lambda: pl.BlockSpec((bb, tm, d), lambda b, i: (b, i, 0))
    full = lambda a: pl.BlockSpec(a.shape, lambda b, i: (0,) * a.ndim)
    hist = pl.BlockSpec((bb, kw - 1, dff), lambda b, i: (b, 0, 0))
    return pl.pallas_call(
        functools.partial(_back_kernel, bb=bb, tm=tm, d=d, dff=dff),
        grid=grid,
        in_specs=[tile(), tile(), tile(), tile(), hist] + [full(a) for a in (wpa, wo, ln2, wup, fw, fb, wdn)],
        out_specs=[tile(), hist],
        out_shape=[jax.ShapeDtypeStruct((nb, t, d), F32), jax.ShapeDtypeStruct((nb, kw - 1, dff), F32)],
        scratch_shapes=[pltpu.VMEM((dff // d, bb, 8 + tm, d), F32), pltpu.VMEM((bb * tm, d), BF16),
                        pltpu.VMEM((bb * tm, d), F32), pltpu.VMEM((bb * tm, d), F32)],
        compiler_params=pltpu.CompilerParams(dimension_semantics=("arbitrary", "arbitrary"),
                                             vmem_limit_bytes=VMEM_LIMIT_BYTES),
        name="back",
    )(x, m, sg, ya, fh, wpa, wo, ln2, wup, fw, fb, wdn)


def _dense_gate_blocks(w):
    nblk, c, _ = w.shape
    per = GATE_GROUP // c
    w4 = w.reshape(nblk // per, per, c, c)
    eye = jnp.eye(per, dtype=w.dtype)
    return jnp.einsum('gncd,nm->gncmd', w4, eye).reshape(nblk // per, GATE_GROUP, GATE_GROUP)


def _layer(x, hist, h0, fh, kc, vc, wts, *, bb, tm, bq, bkd, bkp, heads):
    (ln1, w_in, cw, cb, wg, ba, bx, lam, qg, kg, wpr, wpa, wo, ln2, wup, fw, fb, wdn) = wts
    q, k, v, m, sg, tail, hl = _front(x, hist, h0, ln1, w_in, cw, cb, wg, ba, bx, lam, qg, kg, wpr, bb=bb, tm=tm)
    t = k.shape[1]
    tpad = -(-t // bkd) * bkd
    kn, vn = (k, v) if tpad == t else tuple(jnp.pad(a, ((0, 0), (0, tpad - t), (0, 0))) for a in (k, v))
    ya = _attention(q, kn, vn, kc, vc, bq=bq, bkd=bkd, bkp=bkp, heads=heads)
    y, ft = _back(x, m, sg, ya, fh, wpa, wo, ln2, wup, fw, fb, wdn, bb=bb, tm=tm)
    return y, k, v, tail, hl, ft


def kernel(x_prompt, x_sample, cache_k, cache_v, state_rnn_conv, state_rnn_h, state_ffn_conv, ln1, w_in, rnn_conv_w, rnn_conv_b, lru_wa, lru_ba, lru_wx, lru_bx, lru_lambda, q_norm_g, k_norm_g, w_proj_rnn, w_proj_attn, w_out, ln2, w_up, ffn_conv_w, ffn_conv_b, w_down):
    depth = ln1.shape[0]
    assert depth == 1
    bp, tp, d = x_prompt.shape
    bs, ts, _ = x_sample.shape
    dff = w_down.shape[1]
    row = lambda a: a[0].reshape(1, -1)
    wg = jnp.concatenate([_dense_gate_blocks(lru_wa[0]), _dense_gate_blocks(lru_wx[0])], axis=-1).astype(BF16)
    wts = (row(ln1), w_in[0].astype(BF16), rnn_conv_w[0], row(rnn_conv_b), wg, row(lru_ba), row(lru_bx),
           row(lru_lambda), row(q_norm_g), row(k_norm_g), w_proj_rnn[0].astype(BF16),
           w_proj_attn[0].astype(BF16), w_out[0].astype(BF16), row(ln2), w_up[0].astype(BF16),
           ffn_conv_w[0], row(ffn_conv_b), w_down[0].astype(BF16))
    kw = rnn_conv_w.shape[1]
    fkw = ffn_conv_w.shape[1]

    zc = jnp.zeros((bp, kw - 1, d), F32)
    zh = jnp.zeros((bp, 1, d), F32)
    zf = jnp.zeros((bp, fkw - 1, w_down.shape[1]), F32)
    yp, kp, vp, rcp, hp, fcp = _layer(x_prompt, zc, zh, zf, None, None, wts,
                                      bb=1, tm=256, bq=256, bkd=256, bkp=256, heads=4)
    ys, ks, vs, rcs, hs, fcs = _layer(x_sample, state_rnn_conv[0], state_rnn_h[0].reshape(bs, 1, d),
                                      state_ffn_conv[0], cache_k[0].reshape(bs, -1, d),
                                      cache_v[0].reshape(bs, -1, d), wts,
                                      bb=bs, tm=ts, bq=ts, bkd=128, bkp=256, heads=N_HEADS)
    hd = lambda a: a.reshape(1, a.shape[0], a.shape[1], N_HEADS, HEAD_DIM)
    return (yp, ys, hd(kp), hd(vp), rcp[None], hp.reshape(1, bp, d), fcp[None],
            hd(ks), hd(vs), rcs[None], hs.reshape(1, bs, d), fcs[None])
```

```python
import functools
import math

import jax
import jax.numpy as jnp
from jax import lax
from jax.experimental import pallas as pl
from jax.experimental.pallas import tpu as pltpu

F32 = jnp.float32
BF16 = jnp.bfloat16

NORM_EPS = 1e-6
LRU_C = 8.0
N_HEADS = 8
HEAD_DIM = 128
N_LRU_BLOCKS = 16
LANE = 128
GATE_GROUP = 256
VMEM_LIMIT_BYTES = 56 * 1024 * 1024
SUFFIX_FLOOR = -104.0


def _gelu_tanh(x):
    return 0.5 * x * (1.0 + jnp.tanh(math.sqrt(2.0 / math.pi) * (x + 0.044715 * (x * x * x))))


def _sigmoid(x):
    return 0.5 * jnp.tanh(0.5 * x) + 0.5


def _softplus(x):
    return jnp.maximum(x, 0.0) + jnp.log1p(jnp.exp(-jnp.abs(x)))


def _rms_scale(x):
    return lax.rsqrt(jnp.mean(x * x, axis=-1, keepdims=True) + NORM_EPS)


def _head_rmsnorm(x, g):
    outs = []
    for h in range(N_HEADS):
        xh = x[:, h * HEAD_DIM:(h + 1) * HEAD_DIM]
        outs.append(xh * _rms_scale(xh) * g)
    return outs


def _front_kernel(x_ref, hist_ref, h0_ref, ln1_ref, w_in_ref, cw_ref, cb_ref, wg_ref, ba_ref, bx_ref,
                  lam_ref, qg_ref, kg_ref, wpr_ref,
                  q_ref, k_ref, v_ref, m_ref, sg_ref, tail_ref, hl_ref,
                  u_ref, xp_ref, a_ref, b_ref, hc_ref, *, bb, tm, d):
    i = pl.program_id(1)
    rows = bb * tm
    kw = cw_ref.shape[0]
    pad = 8

    @pl.when(i == 0)
    def _():
        xp_ref[:, pad - (kw - 1):pad, :] = hist_ref[...]
        hc_ref[...] = h0_ref[...]

    x = x_ref[...].reshape(rows, d)
    u_ref[...] = ((x * _rms_scale(x)) * ln1_ref[...]).astype(BF16)

    def proj(c):
        return jnp.dot(u_ref[...], w_in_ref[:, c * d:(c + 1) * d], preferred_element_type=F32)

    xr = proj(0)
    for b in range(bb):
        xp_ref[b, pad:pad + tm, :] = xr[b * tm:(b + 1) * tm]
    xcs = []
    for b in range(bb):
        off = pad - (kw - 1)
        y = xp_ref[b, off:off + tm, :] * cw_ref[0:1, :]
        for j in range(1, kw):
            y = y + xp_ref[b, off + j:off + j + tm, :] * cw_ref[j:j + 1, :]
        xcs.append(y + cb_ref[...])
        tail = xp_ref[b, pad + tm - (kw - 1):pad + tm, :]
        tail_ref[b] = tail
        xp_ref[b, pad - (kw - 1):pad, :] = tail
    xc = xcs[0] if bb == 1 else jnp.concatenate(xcs, axis=0)

    sp_lam = _softplus(-lam_ref[...])
    for g in range(d // GATE_GROUP):
        sl = slice(g * GATE_GROUP, (g + 1) * GATE_GROUP)
        xg = xc[:, sl]
        pre = jnp.dot(xg.astype(BF16), wg_ref[g], preferred_element_type=F32)
        r = _sigmoid(pre[:, :GATE_GROUP] + ba_ref[:, sl])
        ig = _sigmoid(pre[:, GATE_GROUP:] + bx_ref[:, sl])
        log_a = (-LRU_C) * r * sp_lam[:, sl]
        a = jnp.exp(log_a)
        one_m_a2 = -jnp.tanh(log_a) * (a * a + 1.0)
        a_ref[:, sl] = a
        b_ref[:, sl] = jnp.sqrt(one_m_a2) * (ig * xg)

    for b in range(bb):
        def step(t, h, b=b):
            r = b * tm + t
            h = a_ref[pl.ds(r, 1), :] * h + b_ref[pl.ds(r, 1), :]
            b_ref[pl.ds(r, 1), :] = h
            return h
        h_end = lax.fori_loop(0, tm, step, hc_ref[b], unroll=8)
        hc_ref[b] = h_end
        hl_ref[b] = h_end

    y_rnn = (b_ref[...] * _gelu_tanh(proj(1))).astype(BF16)
    m = _sigmoid(proj(5)) * jnp.dot(y_rnn, wpr_ref[...], preferred_element_type=F32)
    m_ref[...] = m.astype(BF16).reshape(bb, tm, d)

    qn = _head_rmsnorm(proj(2), qg_ref[...])
    for h in range(N_HEADS):
        q_ref[:, :, h * HEAD_DIM:(h + 1) * HEAD_DIM] = qn[h].astype(BF16).reshape(bb, tm, HEAD_DIM)
    kn = _head_rmsnorm(proj(3), kg_ref[...])
    for h in range(N_HEADS):
        k_ref[:, :, h * HEAD_DIM:(h + 1) * HEAD_DIM] = kn[h].reshape(bb, tm, HEAD_DIM)
    v_ref[...] = proj(4).reshape(bb, tm, d)
    sg_ref[...] = _sigmoid(proj(6)).astype(BF16).reshape(bb, tm, d)


def _front_chain_kernel(x_ref, hist_ref, h0_ref, ln1_ref, w_in_ref, cw_ref, cb_ref, wg_ref, ba_ref, bx_ref,
                        lam_ref, qg_ref, kg_ref, wpr_ref,
                        q_ref, k_ref, v_ref, m_ref, sg_ref, tail_ref, hl_ref,
                        u_ref, xp_ref, fa_ref, fb_ref, hc_ref, xc_ref, *, tm, d, nsub):
    i = pl.program_id(1)
    rs = tm // nsub
    kw = cw_ref.shape[0]
    pad = 8
    nl = d // LANE
    gl = GATE_GROUP // LANE
    subs = range(nsub)

    def rsl(s):
        return slice(s * rs, (s + 1) * rs)

    @pl.when(i == 0)
    def _():
        xp_ref[0, pad - (kw - 1):pad, :] = hist_ref[0]
        for c in range(nl):
            hc_ref[c:c + 1, :] = h0_ref[0, :, c * LANE:(c + 1) * LANE]

    for s in subs:
        x = x_ref[0, rsl(s), :]
        u_ref[rsl(s), :] = ((x * _rms_scale(x)) * ln1_ref[...]).astype(BF16)

    def proj(c, s):
        return jnp.dot(u_ref[rsl(s), :], w_in_ref[:, c * d:(c + 1) * d], preferred_element_type=F32)

    for s in subs:
        xp_ref[0, pad + s * rs:pad + (s + 1) * rs, :] = proj(0, s)

    ngrp = d // GATE_GROUP
    sp_lam = _softplus(-lam_ref[...])
    gels = [None] * nsub

    def conv(s):
        off = pad + s * rs - (kw - 1)
        xc = xp_ref[0, off:off + rs, :] * cw_ref[0:1, :]
        for j in range(1, kw):
            xc = xc + xp_ref[0, off + j:off + j + rs, :] * cw_ref[j:j + 1, :]
        xc_ref[rsl(s), :] = xc + cb_ref[...]

    def gate(s, g):
        sl = slice(g * GATE_GROUP, (g + 1) * GATE_GROUP)
        xg = xc_ref[rsl(s), sl]
        pre = jnp.dot(xg.astype(BF16), wg_ref[g], preferred_element_type=F32)
        r = _sigmoid(pre[:, :GATE_GROUP] + ba_ref[:, sl])
        ig = _sigmoid(pre[:, GATE_GROUP:] + bx_ref[:, sl])
        log_a = (-LRU_C) * r * sp_lam[:, sl]
        a = jnp.exp(log_a)
        bv = jnp.sqrt(-jnp.tanh(log_a) * (a * a + 1.0)) * (ig * xg)
        for rg in range(rs // 8):
            for cl in range(gl):
                dst = pl.ds((s * rs + 8 * rg) * nl + g * gl + cl, 8, stride=nl)
                fa_ref[dst, :] = a[8 * rg:8 * rg + 8, cl * LANE:(cl + 1) * LANE]
                fb_ref[dst, :] = bv[8 * rg:8 * rg + 8, cl * LANE:(cl + 1) * LANE]

    def keys(s):
        kn = _head_rmsnorm(proj(3, s), kg_ref[...])
        for h in range(N_HEADS):
            k_ref[0, rsl(s), h * HEAD_DIM:(h + 1) * HEAD_DIM] = kn[h]

    def values(s):
        v_ref[0, rsl(s), :] = proj(4, s)

    def queries(s):
        qn = _head_rmsnorm(proj(2, s), qg_ref[...])
        for h in range(N_HEADS):
            q_ref[0, rsl(s), h * HEAD_DIM:(h + 1) * HEAD_DIM] = qn[h].astype(BF16)

    def gelu_gate(s):
        gels[s] = _gelu_tanh(proj(1, s))

    wide = [f for s in subs for f in (keys, values, queries)] + [gelu_gate] * nsub
    wide_args = [s for s in subs for _ in range(3)] + list(subs)
    small = [(s, g) for s in subs for g in range(ngrp)]
    conv(0)
    for n in range(max(len(wide), len(small))):
        if n < len(wide):
            wide[n](wide_args[n])
        if n < len(small):
            s, g = small[n]
            if g == ngrp - 1 and s + 1 < nsub:
                conv(s + 1)
            gate(s, g)
    tail = xp_ref[0, pad + tm - (kw - 1):pad + tm, :]
    tail_ref[0] = tail
    xp_ref[0, pad - (kw - 1):pad, :] = tail

    g_rnns = [_sigmoid(proj(5, s)) for s in subs]
    for s in subs:
        sg_ref[0, rsl(s), :] = _sigmoid(proj(6, s)).astype(BF16)

    h = hc_ref[...]
    for t in range(tm):
        h = fa_ref[t * nl:(t + 1) * nl, :] * h + fb_ref[t * nl:(t + 1) * nl, :]
        fb_ref[t * nl:(t + 1) * nl, :] = h
    hc_ref[...] = h
    for c in range(nl):
        hl_ref[0, :, c * LANE:(c + 1) * LANE] = h[c:c + 1, :]

    for s in subs:
        hseq = jnp.concatenate(
            [jnp.concatenate([fb_ref[pl.ds((s * rs + 8 * rg) * nl + c, 8, stride=nl), :] for c in range(nl)], axis=1)
             for rg in range(rs // 8)], axis=0)
        y_rnn = (hseq * gels[s]).astype(BF16)
        m = g_rnns[s] * jnp.dot(y_rnn, wpr_ref[...], preferred_element_type=F32)
        m_ref[0, rsl(s), :] = m.astype(BF16)


def _front(x, hist, h0, ln1, w_in, cw, cb, wg, ba, bx, lam, qg, kg, wpr, *, bb, tm, nsub):
    nb, t, d = x.shape
    kw = cw.shape[0]
    grid = (nb // bb, t // tm)
    tile = lambda: pl.BlockSpec((bb, tm, d), lambda b, i: (b, i, 0))
    full = lambda a: pl.BlockSpec(a.shape, lambda b, i: (0,) * a.ndim)
    per_b = lambda n: pl.BlockSpec((bb, n, d), lambda b, i: (b, 0, 0))
    act = lambda dt: jax.ShapeDtypeStruct((nb, t, d), dt)
    if bb == 1:
        body = functools.partial(_front_chain_kernel, tm=tm, d=d, nsub=nsub)
        scratch = [pltpu.VMEM((tm, d), BF16), pltpu.VMEM((1, 8 + tm, d), F32),
                   pltpu.VMEM((tm * d // LANE, LANE), F32), pltpu.VMEM((tm * d // LANE, LANE), F32),
                   pltpu.VMEM((d // LANE, LANE), F32), pltpu.VMEM((tm, d), F32)]
    else:
        body = functools.partial(_front_kernel, bb=bb, tm=tm, d=d)
        scratch = [pltpu.VMEM((bb * tm, d), BF16), pltpu.VMEM((bb, 8 + tm, d), F32),
                   pltpu.VMEM((bb * tm, d), F32), pltpu.VMEM((bb * tm, d), F32),
                   pltpu.VMEM((bb, 1, d), F32)]
    return pl.pallas_call(
        body,
        grid=grid,
        in_specs=[tile(), per_b(kw - 1), per_b(1)] + [full(a) for a in (ln1, w_in, cw, cb, wg, ba, bx, lam, qg, kg, wpr)],
        out_specs=[tile(), tile(), tile(), tile(), tile(), per_b(kw - 1), per_b(1)],
        out_shape=[act(BF16), act(F32), act(F32), act(BF16), act(BF16),
                   jax.ShapeDtypeStruct((nb, kw - 1, d), F32), jax.ShapeDtypeStruct((nb, 1, d), F32)],
        scratch_shapes=scratch,
        compiler_params=pltpu.CompilerParams(dimension_semantics=("arbitrary", "arbitrary"),
                                             vmem_limit_bytes=VMEM_LIMIT_BYTES),
        name="front",
    )(x, hist, h0, ln1, w_in, cw, cb, wg, ba, bx, lam, qg, kg, wpr)


def _attn_kernel(*refs, bq, bkd, bkp, heads, has_cache):
    if has_cache:
        q_ref, kn_ref, vn_ref, kp_ref, vp_ref, ud_ref, up_ref, o_ref, acc_ref, r_ref = refs
    else:
        q_ref, kn_ref, vn_ref, ud_ref, up_ref, o_ref, acc_ref, r_ref = refs
        kp_ref, vp_ref = kn_ref, vn_ref
    i = pl.program_id(2)
    scale = HEAD_DIM ** -0.5

    def load_keys(ref, start, bk, h):
        if ref.shape[2] == HEAD_DIM and heads > 1:
            return ref[0, pl.ds(start * N_HEADS + h, bk, stride=N_HEADS), :]
        return ref[0, pl.ds(start, bk), h * HEAD_DIM:(h + 1) * HEAD_DIM]

    def block(k_ref, v_ref, u_ref, start, bk, valid, first):
        hs = range(heads)
        sls = [slice(h * HEAD_DIM, (h + 1) * HEAD_DIM) for h in hs]
        zs = [lax.dot_general(q_ref[0, :, sls[h]], load_keys(k_ref, start, bk, h).astype(BF16),
                              (((1,), (1,)), ((), ())), preferred_element_type=F32) * scale for h in hs]
        log_betas, log_1ms, splits = [], [], []
        for h in hs:
            z = zs[h]
            log_beta = jnp.minimum(z, 0.0) - jnp.log(1.0 + jnp.exp(-jnp.abs(z)))
            log_1m = log_beta - z
            if valid is not None:
                log_1m = jnp.where(valid, log_1m, 0.0)
            hi = log_1m.astype(BF16)
            lo = (log_1m - hi.astype(F32)).astype(BF16)
            log_betas.append(log_beta)
            log_1ms.append(log_1m)
            splits.append(jnp.concatenate([hi, lo], axis=1))
        suffixes = [jnp.dot(splits[h], u_ref[...], preferred_element_type=F32) for h in hs]
        ws, rs_old = [], []
        for h in hs:
            arg = log_betas[h] + suffixes[h]
            if not first:
                r = r_ref[h]
                rs_old.append(r)
                arg = arg + (r if bk == LANE else jnp.concatenate([r] * (bk // LANE), axis=1))
            w = jnp.exp(arg)
            if valid is not None:
                w = jnp.where(valid, w, 0.0)
            ws.append(w.astype(BF16))
        pvs = [jnp.dot(ws[h], load_keys(v_ref, start, bk, h).astype(BF16), preferred_element_type=F32)
               for h in hs]
        for h in hs:
            rs = jnp.broadcast_to(jnp.sum(log_1ms[h], axis=-1, keepdims=True), (bq, LANE))
            if first:
                acc_ref[h] = pvs[h]
                r_ref[h] = rs
            else:
                acc_ref[h] += pvs[h]
                r_ref[h] = rs_old[h] + rs

    def live():
        return jnp.max(r_ref[...]) > SUFFIX_FLOOR

    row = lax.broadcasted_iota(jnp.int32, (bq, bkd), 0)
    col = lax.broadcasted_iota(jnp.int32, (bq, bkd), 1)
    block(kn_ref, vn_ref, ud_ref, pl.multiple_of(i * bq, bq), bkd, col < row, True)

    j0 = jnp.int32(kp_ref.shape[1] // N_HEADS // bkp - 1) if has_cache else i - 1

    def cond(c):
        return jnp.logical_and(c[0] >= 0, c[1])

    def body(c):
        block(kp_ref, vp_ref, up_ref, pl.multiple_of(c[0] * bkp, bkp), bkp, None, False)
        return c[0] - 1, live()
    lax.while_loop(cond, body, (j0, live()))

    for h in range(heads):
        o_ref[0, :, h * HEAD_DIM:(h + 1) * HEAD_DIM] = acc_ref[h].astype(o_ref.dtype)


def _suffix_matrix(bk):
    u = jnp.tril(jnp.ones((bk, bk), F32), -1).astype(BF16)
    return jnp.concatenate([u, u], axis=0)


def _attention(q, kn, vn, kc=None, vc=None, *, bq, bkd, bkp, heads):
    nb, t, d = q.shape
    hw = heads * HEAD_DIM
    has_cache = kc is not None
    assert has_cache or bkd == bkp == bq
    grid = (nb, d // hw, t // bq)
    qspec = pl.BlockSpec((1, bq, hw), lambda b, g, i: (b, i, g))
    kvspec = lambda a: pl.BlockSpec((1, a.shape[1], hw), lambda b, g, i: (b, 0, g))
    full = lambda a: pl.BlockSpec(a.shape, lambda b, g, i: (0,) * a.ndim)
    cachespec = lambda a: pl.BlockSpec((1,) + a.shape[1:], lambda b, g, i: (b, 0, 0))
    assert not has_cache or (heads == N_HEADS and kc.shape[2] == HEAD_DIM)
    kv = [kn, vn] + ([kc, vc] if has_cache else [])
    us = [_suffix_matrix(bkd), _suffix_matrix(bkp)]
    return pl.pallas_call(
        functools.partial(_attn_kernel, bq=bq, bkd=bkd, bkp=bkp, heads=heads, has_cache=has_cache),
        grid=grid,
        in_specs=[qspec, kvspec(kn), kvspec(vn)] + [cachespec(a) for a in kv[2:]] + [full(a) for a in us],
        out_specs=qspec,
        out_shape=jax.ShapeDtypeStruct((nb, t, d), BF16),
        scratch_shapes=[pltpu.VMEM((heads, bq, HEAD_DIM), F32), pltpu.VMEM((heads, bq, LANE), F32)],
        compiler_params=pltpu.CompilerParams(dimension_semantics=("arbitrary", "arbitrary", "arbitrary"),
                                             vmem_limit_bytes=VMEM_LIMIT_BYTES),
        name="attn_cache" if has_cache else "attn",
    )(q, *kv, *us)


def _back_kernel(x_ref, m_ref, sg_ref, ya_ref, fh_ref, wpa_ref, wo_ref, ln2_ref, wup_ref, fw_ref, fb_ref, wdn_ref,
                 y_ref, ft_ref, gp_ref, u2_ref, val_ref, *, bb, tm, d, dff, nsub):
    i = pl.program_id(1)
    rows = bb * tm
    rs = rows // nsub
    kw = fw_ref.shape[0]
    pad = 8
    nchunk = dff // d
    subs = range(nsub)

    def load(ref, s):
        return ref[0, s * rs:(s + 1) * rs, :] if bb == 1 else ref[...].reshape(rows, d)

    def rsl(s):
        return slice(s * rs, (s + 1) * rs)

    @pl.when(i == 0)
    def _():
        for c in range(nchunk):
            gp_ref[c, :, pad - (kw - 1):pad, :] = fh_ref[:, :, c * d:(c + 1) * d]

    atts = [jnp.dot(load(ya_ref, s), wpa_ref[...], preferred_element_type=F32) for s in subs]
    merged = [(load(m_ref, s).astype(F32) + load(sg_ref, s).astype(F32) * atts[s]).astype(BF16) for s in subs]
    outs = [jnp.dot(merged[s], wo_ref[...], preferred_element_type=F32) for s in subs]
    for s in subs:
        x1 = load(x_ref, s) + outs[s]
        if bb == 1:
            y_ref[0, rsl(s), :] = x1
        else:
            y_ref[...] = x1.reshape(bb, tm, d)
        u2_ref[rsl(s), :] = ((x1 * _rms_scale(x1)) * ln2_ref[...]).astype(BF16)

    def up(c):
        for s in subs:
            gp = jnp.dot(u2_ref[rsl(s), :], wup_ref[:, c * d:(c + 1) * d], preferred_element_type=F32)
            if bb == 1:
                gp_ref[c, 0, pad + s * rs:pad + (s + 1) * rs, :] = gp
            else:
                for b in range(bb):
                    gp_ref[c, b, pad:pad + tm, :] = gp[b * tm:(b + 1) * tm]
            val_ref[c % 2, rsl(s), :] = jnp.dot(u2_ref[rsl(s), :], wup_ref[:, dff + c * d:dff + (c + 1) * d],
                                                preferred_element_type=F32)

    def conv(c, b, start, n):
        cs = slice(c * d, (c + 1) * d)
        off = start - (kw - 1)
        g = gp_ref[c, b, off:off + n, :] * fw_ref[0:1, cs]
        for j in range(1, kw):
            g = g + gp_ref[c, b, off + j:off + j + n, :] * fw_ref[j:j + 1, cs]
        return g + fb_ref[:, cs]

    up(0)
    for c in range(nchunk):
        if c + 1 < nchunk:
            up(c + 1)
        cs = slice(c * d, (c + 1) * d)
        for s in subs:
            if bb == 1:
                gc = conv(c, 0, pad + s * rs, rs)
            else:
                gc = jnp.concatenate([conv(c, b, pad, tm) for b in range(bb)], axis=0)
            act = (_gelu_tanh(gc) * val_ref[c % 2, rsl(s), :]).astype(BF16)
            part = jnp.dot(act, wdn_ref[c * d:(c + 1) * d, :], preferred_element_type=F32)
            if bb == 1:
                y_ref[0, rsl(s), :] += part
            else:
                y_ref[...] += part.reshape(bb, tm, d)
        for b in range(bb):
            tail = gp_ref[c, b, pad + tm - (kw - 1):pad + tm, :]
            ft_ref[b, :, cs] = tail
            gp_ref[c, b, pad - (kw - 1):pad, :] = tail


def _back(x, m, sg, ya, fh, wpa, wo, ln2, wup, fw, fb, wdn, *, bb, tm, nsub):
    nb, t, d = x.shape
    dff = wdn.shape[0]
    kw = fw.shape[0]
    assert bb == 1 or nsub == 1
    grid = (nb // bb, t // tm)
    tile = lambda: pl.BlockSpec((bb, tm, d), lambda b, i: (b, i, 0))
    full = lambda a: pl.BlockSpec(a.shape, lambda b, i: (0,) * a.ndim)
    hist = pl.BlockSpec((bb, kw - 1, dff), lambda b, i: (b, 0, 0))
    return pl.pallas_call(
        functools.partial(_back_kernel, bb=bb, tm=tm, d=d, dff=dff, nsub=nsub),
        grid=grid,
        in_specs=[tile(), tile(), tile(), tile(), hist] + [full(a) for a in (wpa, wo, ln2, wup, fw, fb, wdn)],
        out_specs=[tile(), hist],
        out_shape=[jax.ShapeDtypeStruct((nb, t, d), F32), jax.ShapeDtypeStruct((nb, kw - 1, dff), F32)],
        scratch_shapes=[pltpu.VMEM((dff // d, bb, 8 + tm, d), F32), pltpu.VMEM((bb * tm, d), BF16),
                        pltpu.VMEM((2, bb * tm, d), F32)],
        compiler_params=pltpu.CompilerParams(dimension_semantics=("arbitrary", "arbitrary"),
                                             vmem_limit_bytes=VMEM_LIMIT_BYTES),
        name="back",
    )(x, m, sg, ya, fh, wpa, wo, ln2, wup, fw, fb, wdn)


def _dense_gate_blocks(w):
    nblk, c, _ = w.shape
    per = GATE_GROUP // c
    w4 = w.reshape(nblk // per, per, c, c)
    eye = jnp.eye(per, dtype=w.dtype)
    return jnp.einsum('gncd,nm->gncmd', w4, eye).reshape(nblk // per, GATE_GROUP, GATE_GROUP)


def _layer(x, hist, h0, fh, kc, vc, wts, *, bb, tm, tmb, nsub, bq, bkd, bkp, heads):
    (ln1, w_in, cw, cb, wg, ba, bx, lam, qg, kg, wpr, wpa, wo, ln2, wup, fw, fb, wdn) = wts
    q, k, v, m, sg, tail, hl = _front(x, hist, h0, ln1, w_in, cw, cb, wg, ba, bx, lam, qg, kg, wpr,
                                      bb=bb, tm=tm, nsub=nsub)
    t = k.shape[1]
    tpad = -(-t // bkd) * bkd
    kn, vn = (k, v) if tpad == t else tuple(jnp.pad(a, ((0, 0), (0, tpad - t), (0, 0))) for a in (k, v))
    ya = _attention(q, kn, vn, kc, vc, bq=bq, bkd=bkd, bkp=bkp, heads=heads)
    y, ft = _back(x, m, sg, ya, fh, wpa, wo, ln2, wup, fw, fb, wdn, bb=bb, tm=tmb, nsub=nsub)
    return y, k, v, tail, hl, ft


def kernel(x_prompt, x_sample, cache_k, cache_v, state_rnn_conv, state_rnn_h, state_ffn_conv, ln1, w_in, rnn_conv_w, rnn_conv_b, lru_wa, lru_ba, lru_wx, lru_bx, lru_lambda, q_norm_g, k_norm_g, w_proj_rnn, w_proj_attn, w_out, ln2, w_up, ffn_conv_w, ffn_conv_b, w_down):
    depth = ln1.shape[0]
    assert depth == 1
    bp, tp, d = x_prompt.shape
    bs, ts, _ = x_sample.shape
    dff = w_down.shape[1]
    row = lambda a: a[0].reshape(1, -1)
    wg = jnp.concatenate([_dense_gate_blocks(lru_wa[0]), _dense_gate_blocks(lru_wx[0])], axis=-1).astype(BF16)
    wts = (row(ln1), w_in[0].astype(BF16), rnn_conv_w[0], row(rnn_conv_b), wg, row(lru_ba), row(lru_bx),
           row(lru_lambda), row(q_norm_g), row(k_norm_g), w_proj_rnn[0].astype(BF16),
           w_proj_attn[0].astype(BF16), w_out[0].astype(BF16), row(ln2), w_up[0].astype(BF16),
           ffn_conv_w[0], row(ffn_conv_b), w_down[0].astype(BF16))
    kw = rnn_conv_w.shape[1]
    fkw = ffn_conv_w.shape[1]

    zc = jnp.zeros((bp, kw - 1, d), F32)
    zh = jnp.zeros((bp, 1, d), F32)
    zf = jnp.zeros((bp, fkw - 1, w_down.shape[1]), F32)
    yp, kp, vp, rcp, hp, fcp = _layer(x_prompt, zc, zh, zf, None, None, wts,
                                      bb=1, tm=512, tmb=512, nsub=2, bq=256, bkd=256, bkp=256, heads=4)
    ys, ks, vs, rcs, hs, fcs = _layer(x_sample, state_rnn_conv[0], state_rnn_h[0].reshape(bs, 1, d),
                                      state_ffn_conv[0], cache_k[0].reshape(bs, -1, HEAD_DIM),
                                      cache_v[0].reshape(bs, -1, HEAD_DIM), wts,
                                      bb=bs, tm=ts, tmb=ts, nsub=1, bq=ts, bkd=128, bkp=256, heads=N_HEADS)
    hd = lambda a: a.reshape(1, a.shape[0], a.shape[1], N_HEADS, HEAD_DIM)
    return (yp, ys, hd(kp), hd(vp), rcp[None], hp.reshape(1, bp, d), fcp[None],
            hd(ks), hd(vs), rcs[None], hs.reshape(1, bs, d), fcs[None])
```

```python
import functools
import math

import jax
import jax.numpy as jnp
from jax import lax
from jax.experimental import pallas as pl
from jax.experimental.pallas import tpu as pltpu

F32 = jnp.float32
BF16 = jnp.bfloat16

NORM_EPS = 1e-6
LRU_C = 8.0
N_HEADS = 8
HEAD_DIM = 128
N_LRU_BLOCKS = 16
LANE = 128
GATE_GROUP = 256
VMEM_LIMIT_BYTES = 56 * 1024 * 1024
SUFFIX_FLOOR = -104.0


def _gelu_tanh(x):
    return 0.5 * x * (1.0 + jnp.tanh(math.sqrt(2.0 / math.pi) * (x + 0.044715 * (x * x * x))))


def _sigmoid(x):
    return 0.5 * jnp.tanh(0.5 * x) + 0.5


def _softplus(x):
    return jnp.maximum(x, 0.0) + jnp.log1p(jnp.exp(-jnp.abs(x)))


def _rms_scale(x):
    return lax.rsqrt(jnp.mean(x * x, axis=-1, keepdims=True) + NORM_EPS)


def _head_rmsnorm(x, g):
    outs = []
    for h in range(N_HEADS):
        xh = x[:, h * HEAD_DIM:(h + 1) * HEAD_DIM]
        outs.append(xh * _rms_scale(xh) * g)
    return outs


def _front_kernel(x_ref, hist_ref, h0_ref, ln1_ref, w_in_ref, cw_ref, cb_ref, wg_ref, ba_ref, bx_ref,
                  lam_ref, qg_ref, kg_ref, wpr_ref,
                  q_ref, k_ref, v_ref, m_ref, sg_ref, tail_ref, hl_ref,
                  u_ref, xp_ref, a_ref, b_ref, hc_ref, *, bb, tm, d):
    i = pl.program_id(1)
    rows = bb * tm
    kw = cw_ref.shape[0]
    pad = 8

    @pl.when(i == 0)
    def _():
        xp_ref[:, pad - (kw - 1):pad, :] = hist_ref[...]
        hc_ref[...] = h0_ref[...]

    x = x_ref[...].reshape(rows, d)
    u_ref[...] = ((x * _rms_scale(x)) * ln1_ref[...]).astype(BF16)

    def proj(c):
        return jnp.dot(u_ref[...], w_in_ref[:, c * d:(c + 1) * d], preferred_element_type=F32)

    xr = proj(0)
    for b in range(bb):
        xp_ref[b, pad:pad + tm, :] = xr[b * tm:(b + 1) * tm]
    xcs = []
    for b in range(bb):
        off = pad - (kw - 1)
        y = xp_ref[b, off:off + tm, :] * cw_ref[0:1, :]
        for j in range(1, kw):
            y = y + xp_ref[b, off + j:off + j + tm, :] * cw_ref[j:j + 1, :]
        xcs.append(y + cb_ref[...])
        tail = xp_ref[b, pad + tm - (kw - 1):pad + tm, :]
        tail_ref[b] = tail
        xp_ref[b, pad - (kw - 1):pad, :] = tail
    xc = xcs[0] if bb == 1 else jnp.concatenate(xcs, axis=0)

    sp_lam = _softplus(-lam_ref[...])
    for g in range(d // GATE_GROUP):
        sl = slice(g * GATE_GROUP, (g + 1) * GATE_GROUP)
        xg = xc[:, sl]
        pre = jnp.dot(xg.astype(BF16), wg_ref[g], preferred_element_type=F32)
        r = _sigmoid(pre[:, :GATE_GROUP] + ba_ref[:, sl])
        ig = _sigmoid(pre[:, GATE_GROUP:] + bx_ref[:, sl])
        log_a = (-LRU_C) * r * sp_lam[:, sl]
        a = jnp.exp(log_a)
        one_m_a2 = -jnp.tanh(log_a) * (a * a + 1.0)
        a_ref[:, sl] = a
        b_ref[:, sl] = jnp.sqrt(one_m_a2) * (ig * xg)

    for b in range(bb):
        def step(t, h, b=b):
            r = b * tm + t
            h = a_ref[pl.ds(r, 1), :] * h + b_ref[pl.ds(r, 1), :]
            b_ref[pl.ds(r, 1), :] = h
            return h
        h_end = lax.fori_loop(0, tm, step, hc_ref[b], unroll=8)
        hc_ref[b] = h_end
        hl_ref[b] = h_end

    y_rnn = (b_ref[...] * _gelu_tanh(proj(1))).astype(BF16)
    m = _sigmoid(proj(5)) * jnp.dot(y_rnn, wpr_ref[...], preferred_element_type=F32)
    m_ref[...] = m.astype(BF16).reshape(bb, tm, d)

    qn = _head_rmsnorm(proj(2), qg_ref[...])
    for h in range(N_HEADS):
        q_ref[:, :, h * HEAD_DIM:(h + 1) * HEAD_DIM] = qn[h].astype(BF16).reshape(bb, tm, HEAD_DIM)
    kn = _head_rmsnorm(proj(3), kg_ref[...])
    for h in range(N_HEADS):
        k_ref[:, :, h * HEAD_DIM:(h + 1) * HEAD_DIM] = kn[h].reshape(bb, tm, HEAD_DIM)
    v_ref[...] = proj(4).reshape(bb, tm, d)
    sg_ref[...] = _sigmoid(proj(6)).astype(BF16).reshape(bb, tm, d)


def _front_chain_kernel(x_ref, hist_ref, h0_ref, ln1_ref, w_in_ref, cw_ref, cb_ref, wg_ref, ba_ref, bx_ref,
                        lam_ref, qg_ref, kg_ref, wpr_ref,
                        q_ref, k_ref, v_ref, m_ref, sg_ref, tail_ref, hl_ref,
                        u_ref, xp_ref, fa_ref, fb_ref, hc_ref, xc_ref, *, tm, d, nsub):
    i = pl.program_id(1)
    rs = tm // nsub
    kw = cw_ref.shape[0]
    pad = 8
    nl = d // LANE
    gl = GATE_GROUP // LANE
    subs = range(nsub)

    def rsl(s):
        return slice(s * rs, (s + 1) * rs)

    @pl.when(i == 0)
    def _():
        xp_ref[0, pad - (kw - 1):pad, :] = hist_ref[0]
        for c in range(nl):
            hc_ref[c:c + 1, :] = h0_ref[0, :, c * LANE:(c + 1) * LANE]

    for s in subs:
        x = x_ref[0, rsl(s), :]
        u_ref[rsl(s), :] = ((x * _rms_scale(x)) * ln1_ref[...]).astype(BF16)

    def proj(c, s):
        return jnp.dot(u_ref[rsl(s), :], w_in_ref[:, c * d:(c + 1) * d], preferred_element_type=F32)

    for s in subs:
        xp_ref[0, pad + s * rs:pad + (s + 1) * rs, :] = proj(0, s)

    ngrp = d // GATE_GROUP
    sp_lam = _softplus(-lam_ref[...])
    gels = [None] * nsub

    def conv(s):
        off = pad + s * rs - (kw - 1)
        xc = xp_ref[0, off:off + rs, :] * cw_ref[0:1, :]
        for j in range(1, kw):
            xc = xc + xp_ref[0, off + j:off + j + rs, :] * cw_ref[j:j + 1, :]
        xc_ref[rsl(s), :] = xc + cb_ref[...]

    def gate(s, g):
        sl = slice(g * GATE_GROUP, (g + 1) * GATE_GROUP)
        xg = xc_ref[rsl(s), sl]
        pre = jnp.dot(xg.astype(BF16), wg_ref[g], preferred_element_type=F32)
        r = _sigmoid(pre[:, :GATE_GROUP] + ba_ref[:, sl])
        ig = _sigmoid(pre[:, GATE_GROUP:] + bx_ref[:, sl])
        log_a = (-LRU_C) * r * sp_lam[:, sl]
        a = jnp.exp(log_a)
        bv = jnp.sqrt(-jnp.tanh(log_a) * (a * a + 1.0)) * (ig * xg)
        for rg in range(rs // 8):
            for cl in range(gl):
                dst = pl.ds((s * rs + 8 * rg) * nl + g * gl + cl, 8, stride=nl)
                fa_ref[dst, :] = a[8 * rg:8 * rg + 8, cl * LANE:(cl + 1) * LANE]
                fb_ref[dst, :] = bv[8 * rg:8 * rg + 8, cl * LANE:(cl + 1) * LANE]

    def keys(s):
        kn = _head_rmsnorm(proj(3, s), kg_ref[...])
        for h in range(N_HEADS):
            k_ref[0, rsl(s), h * HEAD_DIM:(h + 1) * HEAD_DIM] = kn[h]

    def values(s):
        v_ref[0, rsl(s), :] = proj(4, s)

    def queries(s):
        qn = _head_rmsnorm(proj(2, s), qg_ref[...])
        for h in range(N_HEADS):
            q_ref[0, rsl(s), h * HEAD_DIM:(h + 1) * HEAD_DIM] = qn[h].astype(BF16)

    def gelu_gate(s):
        gels[s] = _gelu_tanh(proj(1, s))

    wide = [f for s in subs for f in (keys, values, queries)] + [gelu_gate] * nsub
    wide_args = [s for s in subs for _ in range(3)] + list(subs)
    small = [(s, g) for s in subs for g in range(ngrp)]
    conv(0)
    for n in range(max(len(wide), len(small))):
        if n < len(wide):
            wide[n](wide_args[n])
        if n < len(small):
            s, g = small[n]
            if g == ngrp - 1 and s + 1 < nsub:
                conv(s + 1)
            gate(s, g)
    tail = xp_ref[0, pad + tm - (kw - 1):pad + tm, :]
    tail_ref[0] = tail
    xp_ref[0, pad - (kw - 1):pad, :] = tail

    g_rnns = [_sigmoid(proj(5, s)) for s in subs]
    for s in subs:
        sg_ref[0, rsl(s), :] = _sigmoid(proj(6, s)).astype(BF16)

    h = hc_ref[...]
    for t in range(tm):
        h = fa_ref[t * nl:(t + 1) * nl, :] * h + fb_ref[t * nl:(t + 1) * nl, :]
        fb_ref[t * nl:(t + 1) * nl, :] = h
    hc_ref[...] = h
    for c in range(nl):
        hl_ref[0, :, c * LANE:(c + 1) * LANE] = h[c:c + 1, :]

    for s in subs:
        hseq = jnp.concatenate(
            [jnp.concatenate([fb_ref[pl.ds((s * rs + 8 * rg) * nl + c, 8, stride=nl), :] for c in range(nl)], axis=1)
             for rg in range(rs // 8)], axis=0)
        y_rnn = (hseq * gels[s]).astype(BF16)
        m = g_rnns[s] * jnp.dot(y_rnn, wpr_ref[...], preferred_element_type=F32)
        m_ref[0, rsl(s), :] = m.astype(BF16)


def _front(x, hist, h0, ln1, w_in, cw, cb, wg, ba, bx, lam, qg, kg, wpr, *, bb, tm, nsub):
    nb, t, d = x.shape
    kw = cw.shape[0]
    grid = (nb // bb, t // tm)
    tile = lambda: pl.BlockSpec((bb, tm, d), lambda b, i: (b, i, 0))
    full = lambda a: pl.BlockSpec(a.shape, lambda b, i: (0,) * a.ndim)
    per_b = lambda n: pl.BlockSpec((bb, n, d), lambda b, i: (b, 0, 0))
    act = lambda dt: jax.ShapeDtypeStruct((nb, t, d), dt)
    if bb == 1:
        body = functools.partial(_front_chain_kernel, tm=tm, d=d, nsub=nsub)
        scratch = [pltpu.VMEM((tm, d), BF16), pltpu.VMEM((1, 8 + tm, d), F32),
                   pltpu.VMEM((tm * d // LANE, LANE), F32), pltpu.VMEM((tm * d // LANE, LANE), F32),
                   pltpu.VMEM((d // LANE, LANE), F32), pltpu.VMEM((tm, d), F32)]
    else:
        body = functools.partial(_front_kernel, bb=bb, tm=tm, d=d)
        scratch = [pltpu.VMEM((bb * tm, d), BF16), pltpu.VMEM((bb, 8 + tm, d), F32),
                   pltpu.VMEM((bb * tm, d), F32), pltpu.VMEM((bb * tm, d), F32),
                   pltpu.VMEM((bb, 1, d), F32)]
    return pl.pallas_call(
        body,
        grid=grid,
        in_specs=[tile(), per_b(kw - 1), per_b(1)] + [full(a) for a in (ln1, w_in, cw, cb, wg, ba, bx, lam, qg, kg, wpr)],
        out_specs=[tile(), tile(), tile(), tile(), tile(), per_b(kw - 1), per_b(1)],
        out_shape=[act(BF16), act(F32), act(F32), act(BF16), act(BF16),
                   jax.ShapeDtypeStruct((nb, kw - 1, d), F32), jax.ShapeDtypeStruct((nb, 1, d), F32)],
        scratch_shapes=scratch,
        compiler_params=pltpu.CompilerParams(dimension_semantics=("arbitrary", "arbitrary"),
                                             vmem_limit_bytes=VMEM_LIMIT_BYTES),
        name="front",
    )(x, hist, h0, ln1, w_in, cw, cb, wg, ba, bx, lam, qg, kg, wpr)


def _attn_kernel(*refs, bq, bk, heads, has_cache):
    if has_cache:
        q_ref, kn_ref, vn_ref, kp_ref, vp_ref, u_ref, o_ref, acc_ref, r_ref, kpad_ref, vpad_ref = refs
    else:
        q_ref, kn_ref, vn_ref, u_ref, o_ref, acc_ref, r_ref = refs
        kp_ref, vp_ref = kn_ref, vn_ref
    i = pl.program_id(2)
    scale = HEAD_DIM ** -0.5
    rh = min(bq, LANE)
    ngrp = bq // rh
    chains = [(h, g) for h in range(heads) for g in range(ngrp)]

    def head_rows(ref, start, n, h):
        if ref.shape[-1] == HEAD_DIM and heads > 1:
            return ref[0, pl.ds(start * N_HEADS + h, n, stride=N_HEADS), :]
        lanes = slice(h * HEAD_DIM, (h + 1) * HEAD_DIM)
        return ref[pl.ds(start, n), lanes] if len(ref.shape) == 2 else ref[0, pl.ds(start, n), lanes]

    def run(tiles, state):
        zs = []
        for c, k_ref, _, start, nk, _ in tiles:
            h, g = chains[c]
            qh = q_ref[0, g * rh:(g + 1) * rh, h * HEAD_DIM:(h + 1) * HEAD_DIM]
            zs.append(lax.dot_general(qh, head_rows(k_ref, start, nk, h).astype(BF16), (((1,), (1,)), ((), ())),
                                      preferred_element_type=F32) * scale)
        log_betas, log_1ms = [], []
        for (_, _, _, _, _, valid), z in zip(tiles, zs):
            log_beta = jnp.minimum(z, 0.0) - jnp.log(1.0 + jnp.exp(-jnp.abs(z)))
            log_1m = log_beta - z
            log_betas.append(log_beta)
            log_1ms.append(log_1m if valid is None else jnp.where(valid, log_1m, 0.0))
        suffixes = [jnp.dot(l.astype(BF16), u_ref[0:t[4], 0:t[4]], preferred_element_type=F32)
                    for t, l in zip(tiles, log_1ms)]
        state = list(state)
        ws = []
        for n, (c, _, _, _, nk, valid) in enumerate(tiles):
            arg = log_betas[n] + suffixes[n]
            rs = jnp.broadcast_to(jnp.sum(log_1ms[n], axis=-1, keepdims=True), (rh, LANE))
            if state[c] is None:
                state[c] = (rs, None)
            else:
                r, acc = state[c]
                arg = arg + (r if nk == LANE else jnp.concatenate([r] * (nk // LANE), axis=1))
                state[c] = (r + rs, acc)
            w = jnp.exp(arg)
            ws.append((w if valid is None else jnp.where(valid, w, 0.0)).astype(BF16))
        for n, (c, _, v_ref, start, nk, _) in enumerate(tiles):
            pv = jnp.dot(ws[n], head_rows(v_ref, start, nk, chains[c][0]).astype(BF16), preferred_element_type=F32)
            r, acc = state[c]
            state[c] = (r, pv if acc is None else acc + pv)
        return state

    def save(state):
        for c, entry in enumerate(state):
            if entry is not None:
                r_ref[c], acc_ref[c] = entry

    def load(only=None):
        return [(r_ref[c], acc_ref[c]) if only is None or only(g) else None for c, (_, g) in enumerate(chains)]

    def live():
        return jnp.max(r_ref[...]) > SUFFIX_FLOOR

    if has_cache:
        tn = kn_ref.shape[1]
        for pad_ref, src_ref in ((kpad_ref, kn_ref), (vpad_ref, vn_ref)):
            pad_ref[...] = jnp.zeros_like(pad_ref)
            pad_ref[0:tn, :] = src_ref[0]
        kd_ref, vd_ref, dstart = kpad_ref, vpad_ref, 0
    else:
        kd_ref, vd_ref, dstart = kn_ref, vn_ref, pl.multiple_of(i * bq, bq)
    diag = []
    for c, (h, g) in enumerate(chains):
        nk = -(-((g + 1) * rh) // LANE) * LANE
        row = lax.broadcasted_iota(jnp.int32, (rh, nk), 0) + g * rh
        col = lax.broadcasted_iota(jnp.int32, (rh, nk), 1)
        diag.append((c, kd_ref, vd_ref, dstart, nk, col < row))

    def past(j):
        start = j * bk if isinstance(j, int) else pl.multiple_of(j * bk, bk)
        return [(c, kp_ref, vp_ref, start, bk, None) for c in range(len(chains))]

    fresh = [None] * len(chains)
    if has_cache:
        j0 = kp_ref.shape[1] // N_HEADS // bk - 1
        save(run(diag + past(j0), fresh))
        j1 = jnp.int32(j0 - 1)
    else:
        def earlier(lo, n):
            return pl.multiple_of(dstart - bk + lo, LANE), n

        @pl.when(i > 0)
        def _():
            save(run(diag + [(c, kp_ref, vp_ref) + earlier(g * rh, bk - g * rh) + (None,)
                             for c, (_, g) in enumerate(chains)], fresh))

        @pl.when(i == 0)
        def _():
            save(run(diag, fresh))

        if ngrp > 1:
            @pl.when(jnp.logical_and(i > 0, live()))
            def _():
                save(run([(c, kp_ref, vp_ref) + earlier(0, g * rh) + (None,)
                          for c, (_, g) in enumerate(chains) if g > 0], load(lambda g: g > 0)))
        j1 = i - 2

    def cond(c):
        return jnp.logical_and(c[0] >= 0, c[1])

    def body(c):
        save(run(past(c[0]), load()))
        return c[0] - 1, live()
    lax.while_loop(cond, body, (j1, live()))

    for c, (h, g) in enumerate(chains):
        o_ref[0, g * rh:(g + 1) * rh, h * HEAD_DIM:(h + 1) * HEAD_DIM] = acc_ref[c].astype(o_ref.dtype)


def _attention(q, kn, vn, kc=None, vc=None, *, bq, bk, heads):
    nb, t, d = q.shape
    hw = heads * HEAD_DIM
    has_cache = kc is not None
    assert has_cache or (bk == bq and bq % LANE == 0)
    assert not has_cache or (heads == N_HEADS and kc.shape[2] == HEAD_DIM and t == bq <= LANE)
    rh = min(bq, LANE)
    nchain = heads * (bq // rh)
    grid = (nb, d // hw, t // bq)
    qspec = pl.BlockSpec((1, bq, hw), lambda b, g, i: (b, i, g))
    kvspec = lambda a: pl.BlockSpec((1, a.shape[1], hw), lambda b, g, i: (b, 0, g))
    cachespec = lambda a: pl.BlockSpec((1,) + a.shape[1:], lambda b, g, i: (b, 0, 0))
    u = jnp.tril(jnp.ones((bk, bk), F32), -1).astype(BF16)
    kv = [kn, vn] + ([kc, vc] if has_cache else [])
    scratch = [pltpu.VMEM((nchain, rh, HEAD_DIM), F32), pltpu.VMEM((nchain, rh, LANE), F32)]
    if has_cache:
        scratch += [pltpu.VMEM((LANE, hw), F32), pltpu.VMEM((LANE, hw), F32)]
    return pl.pallas_call(
        functools.partial(_attn_kernel, bq=bq, bk=bk, heads=heads, has_cache=has_cache),
        grid=grid,
        in_specs=[qspec, kvspec(kn), kvspec(vn)] + [cachespec(a) for a in kv[2:]]
        + [pl.BlockSpec(u.shape, lambda b, g, i: (0, 0))],
        out_specs=qspec,
        out_shape=jax.ShapeDtypeStruct((nb, t, d), BF16),
        scratch_shapes=scratch,
        compiler_params=pltpu.CompilerParams(dimension_semantics=("arbitrary", "arbitrary", "arbitrary"),
                                             vmem_limit_bytes=VMEM_LIMIT_BYTES),
        name="attn_cache" if has_cache else "attn",
    )(q, *kv, u)


def _back_kernel(x_ref, m_ref, sg_ref, ya_ref, fh_ref, wpa_ref, wo_ref, ln2_ref, wup_ref, fw_ref, fb_ref, wdn_ref,
                 y_ref, ft_ref, gp_ref, u2_ref, val_ref, *, bb, tm, d, dff, nsub):
    i = pl.program_id(1)
    rows = bb * tm
    rs = rows // nsub
    kw = fw_ref.shape[0]
    pad = 8
    nchunk = dff // d
    subs = range(nsub)

    def load(ref, s):
        return ref[0, s * rs:(s + 1) * rs, :] if bb == 1 else ref[...].reshape(rows, d)

    def rsl(s):
        return slice(s * rs, (s + 1) * rs)

    @pl.when(i == 0)
    def _():
        for c in range(nchunk):
            gp_ref[c, :, pad - (kw - 1):pad, :] = fh_ref[:, :, c * d:(c + 1) * d]

    atts = [jnp.dot(load(ya_ref, s), wpa_ref[...], preferred_element_type=F32) for s in subs]
    merged = [(load(m_ref, s).astype(F32) + load(sg_ref, s).astype(F32) * atts[s]).astype(BF16) for s in subs]
    outs = [jnp.dot(merged[s], wo_ref[...], preferred_element_type=F32) for s in subs]
    for s in subs:
        x1 = load(x_ref, s) + outs[s]
        if bb == 1:
            y_ref[0, rsl(s), :] = x1
        else:
            y_ref[...] = x1.reshape(bb, tm, d)
        u2_ref[rsl(s), :] = ((x1 * _rms_scale(x1)) * ln2_ref[...]).astype(BF16)

    def up(c):
        for s in subs:
            gp = jnp.dot(u2_ref[rsl(s), :], wup_ref[:, c * d:(c + 1) * d], preferred_element_type=F32)
            if bb == 1:
                gp_ref[c, 0, pad + s * rs:pad + (s + 1) * rs, :] = gp
            else:
                for b in range(bb):
                    gp_ref[c, b, pad:pad + tm, :] = gp[b * tm:(b + 1) * tm]
            val_ref[c % 2, rsl(s), :] = jnp.dot(u2_ref[rsl(s), :], wup_ref[:, dff + c * d:dff + (c + 1) * d],
                                                preferred_element_type=F32)

    def conv(c, b, start, n):
        cs = slice(c * d, (c + 1) * d)
        off = start - (kw - 1)
        g = gp_ref[c, b, off:off + n, :] * fw_ref[0:1, cs]
        for j in range(1, kw):
            g = g + gp_ref[c, b, off + j:off + j + n, :] * fw_ref[j:j + 1, cs]
        return g + fb_ref[:, cs]

    up(0)
    for c in range(nchunk):
        if c + 1 < nchunk:
            up(c + 1)
        cs = slice(c * d, (c + 1) * d)
        for s in subs:
            if bb == 1:
                gc = conv(c, 0, pad + s * rs, rs)
            else:
                gc = jnp.concatenate([conv(c, b, pad, tm) for b in range(bb)], axis=0)
            act = (_gelu_tanh(gc) * val_ref[c % 2, rsl(s), :]).astype(BF16)
            part = jnp.dot(act, wdn_ref[c * d:(c + 1) * d, :], preferred_element_type=F32)
            if bb == 1:
                y_ref[0, rsl(s), :] += part
            else:
                y_ref[...] += part.reshape(bb, tm, d)
        for b in range(bb):
            tail = gp_ref[c, b, pad + tm - (kw - 1):pad + tm, :]
            ft_ref[b, :, cs] = tail
            gp_ref[c, b, pad - (kw - 1):pad, :] = tail


def _back(x, m, sg, ya, fh, wpa, wo, ln2, wup, fw, fb, wdn, *, bb, tm, nsub):
    nb, t, d = x.shape
    dff = wdn.shape[0]
    kw = fw.shape[0]
    assert bb == 1 or nsub == 1
    grid = (nb // bb, t // tm)
    tile = lambda: pl.BlockSpec((bb, tm, d), lambda b, i: (b, i, 0))
    full = lambda a: pl.BlockSpec(a.shape, lambda b, i: (0,) * a.ndim)
    hist = pl.BlockSpec((bb, kw - 1, dff), lambda b, i: (b, 0, 0))
    return pl.pallas_call(
        functools.partial(_back_kernel, bb=bb, tm=tm, d=d, dff=dff, nsub=nsub),
        grid=grid,
        in_specs=[tile(), tile(), tile(), tile(), hist] + [full(a) for a in (wpa, wo, ln2, wup, fw, fb, wdn)],
        out_specs=[tile(), hist],
        out_shape=[jax.ShapeDtypeStruct((nb, t, d), F32), jax.ShapeDtypeStruct((nb, kw - 1, dff), F32)],
        scratch_shapes=[pltpu.VMEM((dff // d, bb, 8 + tm, d), F32), pltpu.VMEM((bb * tm, d), BF16),
                        pltpu.VMEM((2, bb * tm, d), F32)],
        compiler_params=pltpu.CompilerParams(dimension_semantics=("arbitrary", "arbitrary"),
                                             vmem_limit_bytes=VMEM_LIMIT_BYTES),
        name="back",
    )(x, m, sg, ya, fh, wpa, wo, ln2, wup, fw, fb, wdn)


def _dense_gate_blocks(w):
    nblk, c, _ = w.shape
    per = GATE_GROUP // c
    w4 = w.reshape(nblk // per, per, c, c)
    eye = jnp.eye(per, dtype=w.dtype)
    return jnp.einsum('gncd,nm->gncmd', w4, eye).reshape(nblk // per, GATE_GROUP, GATE_GROUP)


def _layer(x, hist, h0, fh, kc, vc, wts, *, bb, tm, tmb, nsub, bq, bk, heads):
    (ln1, w_in, cw, cb, wg, ba, bx, lam, qg, kg, wpr, wpa, wo, ln2, wup, fw, fb, wdn) = wts
    q, k, v, m, sg, tail, hl = _front(x, hist, h0, ln1, w_in, cw, cb, wg, ba, bx, lam, qg, kg, wpr,
                                      bb=bb, tm=tm, nsub=nsub)
    ya = _attention(q, k, v, kc, vc, bq=bq, bk=bk, heads=heads)
    y, ft = _back(x, m, sg, ya, fh, wpa, wo, ln2, wup, fw, fb, wdn, bb=bb, tm=tmb, nsub=nsub)
    return y, k, v, tail, hl, ft


def kernel(x_prompt, x_sample, cache_k, cache_v, state_rnn_conv, state_rnn_h, state_ffn_conv, ln1, w_in, rnn_conv_w, rnn_conv_b, lru_wa, lru_ba, lru_wx, lru_bx, lru_lambda, q_norm_g, k_norm_g, w_proj_rnn, w_proj_attn, w_out, ln2, w_up, ffn_conv_w, ffn_conv_b, w_down):
    depth = ln1.shape[0]
    assert depth == 1
    bp, tp, d = x_prompt.shape
    bs, ts, _ = x_sample.shape
    dff = w_down.shape[1]
    row = lambda a: a[0].reshape(1, -1)
    wg = jnp.concatenate([_dense_gate_blocks(lru_wa[0]), _dense_gate_blocks(lru_wx[0])], axis=-1).astype(BF16)
    wts = (row(ln1), w_in[0].astype(BF16), rnn_conv_w[0], row(rnn_conv_b), wg, row(lru_ba), row(lru_bx),
           row(lru_lambda), row(q_norm_g), row(k_norm_g), w_proj_rnn[0].astype(BF16),
           w_proj_attn[0].astype(BF16), w_out[0].astype(BF16), row(ln2), w_up[0].astype(BF16),
           ffn_conv_w[0], row(ffn_conv_b), w_down[0].astype(BF16))
    kw = rnn_conv_w.shape[1]
    fkw = ffn_conv_w.shape[1]

    zc = jnp.zeros((bp, kw - 1, d), F32)
    zh = jnp.zeros((bp, 1, d), F32)
    zf = jnp.zeros((bp, fkw - 1, w_down.shape[1]), F32)
    yp, kp, vp, rcp, hp, fcp = _layer(x_prompt, zc, zh, zf, None, None, wts,
                                      bb=1, tm=512, tmb=512, nsub=2, bq=256, bk=256, heads=4)
    ys, ks, vs, rcs, hs, fcs = _layer(x_sample, state_rnn_conv[0], state_rnn_h[0].reshape(bs, 1, d),
                                      state_ffn_conv[0], cache_k[0].reshape(bs, -1, HEAD_DIM),
                                      cache_v[0].reshape(bs, -1, HEAD_DIM), wts,
                                      bb=bs, tm=ts, tmb=ts, nsub=1, bq=ts, bk=256, heads=N_HEADS)
    hd = lambda a: a.reshape(1, a.shape[0], a.shape[1], N_HEADS, HEAD_DIM)
    return (yp, ys, hd(kp), hd(vp), rcp[None], hp.reshape(1, bp, d), fcp[None],
            hd(ks), hd(vs), rcs[None], hs.reshape(1, bs, d), fcs[None])
```

```python
import functools
import math

import jax
import jax.numpy as jnp
from jax import lax
from jax.experimental import pallas as pl
from jax.experimental.pallas import tpu as pltpu

F32 = jnp.float32
BF16 = jnp.bfloat16

NORM_EPS = 1e-6
LRU_C = 8.0
N_HEADS = 8
HEAD_DIM = 128
N_LRU_BLOCKS = 16
LANE = 128
GATE_GROUP = 256
VMEM_LIMIT_BYTES = 56 * 1024 * 1024
SUFFIX_FLOOR = -104.0


def _gelu_tanh(x):
    return 0.5 * x * (1.0 + jnp.tanh(math.sqrt(2.0 / math.pi) * (x + 0.044715 * (x * x * x))))


def _sigmoid(x):
    return 0.5 * jnp.tanh(0.5 * x) + 0.5


def _softplus(x):
    return jnp.maximum(x, 0.0) + jnp.log1p(jnp.exp(-jnp.abs(x)))


def _rms_scale(x):
    return lax.rsqrt(jnp.mean(x * x, axis=-1, keepdims=True) + NORM_EPS)


def _head_rmsnorm(x, g):
    outs = []
    for h in range(N_HEADS):
        xh = x[:, h * HEAD_DIM:(h + 1) * HEAD_DIM]
        outs.append(xh * _rms_scale(xh) * g)
    return outs


def _front_kernel(x_ref, hist_ref, h0_ref, ln1_ref, w_in_ref, cw_ref, cb_ref, wg_ref, ba_ref, bx_ref,
                  lam_ref, qg_ref, kg_ref, wpr_ref,
                  q_ref, k_ref, v_ref, m_ref, sg_ref, tail_ref, hl_ref,
                  u_ref, xp_ref, a_ref, b_ref, hc_ref, *, bb, tm, d):
    i = pl.program_id(1)
    rows = bb * tm
    kw = cw_ref.shape[0]
    pad = 8

    @pl.when(i == 0)
    def _():
        xp_ref[:, pad - (kw - 1):pad, :] = hist_ref[...]
        hc_ref[...] = h0_ref[...]

    x = x_ref[...].reshape(rows, d)
    u_ref[...] = ((x * _rms_scale(x)) * ln1_ref[...]).astype(BF16)

    def proj(c):
        return jnp.dot(u_ref[...], w_in_ref[:, c * d:(c + 1) * d], preferred_element_type=F32)

    xr = proj(0)
    for b in range(bb):
        xp_ref[b, pad:pad + tm, :] = xr[b * tm:(b + 1) * tm]
    xcs = []
    for b in range(bb):
        off = pad - (kw - 1)
        y = xp_ref[b, off:off + tm, :] * cw_ref[0:1, :]
        for j in range(1, kw):
            y = y + xp_ref[b, off + j:off + j + tm, :] * cw_ref[j:j + 1, :]
        xcs.append(y + cb_ref[...])
        tail = xp_ref[b, pad + tm - (kw - 1):pad + tm, :]
        tail_ref[b] = tail
        xp_ref[b, pad - (kw - 1):pad, :] = tail
    xc = xcs[0] if bb == 1 else jnp.concatenate(xcs, axis=0)

    sp_lam = _softplus(-lam_ref[...])
    for g in range(d // GATE_GROUP):
        sl = slice(g * GATE_GROUP, (g + 1) * GATE_GROUP)
        xg = xc[:, sl]
        pre = jnp.dot(xg.astype(BF16), wg_ref[g], preferred_element_type=F32)
        r = _sigmoid(pre[:, :GATE_GROUP] + ba_ref[:, sl])
        ig = _sigmoid(pre[:, GATE_GROUP:] + bx_ref[:, sl])
        log_a = (-LRU_C) * r * sp_lam[:, sl]
        a = jnp.exp(log_a)
        one_m_a2 = -jnp.tanh(log_a) * (a * a + 1.0)
        a_ref[:, sl] = a
        b_ref[:, sl] = jnp.sqrt(one_m_a2) * (ig * xg)

    for b in range(bb):
        def step(t, h, b=b):
            r = b * tm + t
            h = a_ref[pl.ds(r, 1), :] * h + b_ref[pl.ds(r, 1), :]
            b_ref[pl.ds(r, 1), :] = h
            return h
        h_end = lax.fori_loop(0, tm, step, hc_ref[b], unroll=8)
        hc_ref[b] = h_end
        hl_ref[b] = h_end

    y_rnn = (b_ref[...] * _gelu_tanh(proj(1))).astype(BF16)
    m = _sigmoid(proj(5)) * jnp.dot(y_rnn, wpr_ref[...], preferred_element_type=F32)
    m_ref[...] = m.astype(BF16).reshape(bb, tm, d)

    qn = _head_rmsnorm(proj(2), qg_ref[...])
    for h in range(N_HEADS):
        q_ref[:, :, h * HEAD_DIM:(h + 1) * HEAD_DIM] = qn[h].astype(BF16).reshape(bb, tm, HEAD_DIM)
    kn = _head_rmsnorm(proj(3), kg_ref[...])
    for h in range(N_HEADS):
        k_ref[:, :, h * HEAD_DIM:(h + 1) * HEAD_DIM] = kn[h].reshape(bb, tm, HEAD_DIM)
    v_ref[...] = proj(4).reshape(bb, tm, d)
    sg_ref[...] = _sigmoid(proj(6)).astype(BF16).reshape(bb, tm, d)


def _front_chain_kernel(x_ref, hist_ref, h0_ref, ln1_ref, w_in_ref, cw_ref, cb_ref, wg_ref, ba_ref, bx_ref,
                        lam_ref, qg_ref, kg_ref, wpr_ref,
                        q_ref, k_ref, v_ref, m_ref, sg_ref, tail_ref, hl_ref,
                        u_ref, xp_ref, fa_ref, fb_ref, hc_ref, xc_ref, *, tm, d, nsub):
    i = pl.program_id(1)
    rs = tm // nsub
    kw = cw_ref.shape[0]
    pad = 8
    nl = d // LANE
    gl = GATE_GROUP // LANE
    subs = range(nsub)

    def rsl(s):
        return slice(s * rs, (s + 1) * rs)

    @pl.when(i == 0)
    def _():
        xp_ref[0, pad - (kw - 1):pad, :] = hist_ref[0]
        for c in range(nl):
            hc_ref[c:c + 1, :] = h0_ref[0, :, c * LANE:(c + 1) * LANE]

    for s in subs:
        x = x_ref[0, rsl(s), :]
        u_ref[rsl(s), :] = ((x * _rms_scale(x)) * ln1_ref[...]).astype(BF16)

    def proj(c, s):
        return jnp.dot(u_ref[rsl(s), :], w_in_ref[:, c * d:(c + 1) * d], preferred_element_type=F32)

    for s in subs:
        xp_ref[0, pad + s * rs:pad + (s + 1) * rs, :] = proj(0, s)

    ngrp = d // GATE_GROUP
    sp_lam = _softplus(-lam_ref[...])
    gels = [None] * nsub

    def conv(s):
        off = pad + s * rs - (kw - 1)
        xc = xp_ref[0, off:off + rs, :] * cw_ref[0:1, :]
        for j in range(1, kw):
            xc = xc + xp_ref[0, off + j:off + j + rs, :] * cw_ref[j:j + 1, :]
        xc_ref[rsl(s), :] = xc + cb_ref[...]

    def gate(s, g):
        sl = slice(g * GATE_GROUP, (g + 1) * GATE_GROUP)
        xg = xc_ref[rsl(s), sl]
        pre = jnp.dot(xg.astype(BF16), wg_ref[g], preferred_element_type=F32)
        r = _sigmoid(pre[:, :GATE_GROUP] + ba_ref[:, sl])
        ig = _sigmoid(pre[:, GATE_GROUP:] + bx_ref[:, sl])
        log_a = (-LRU_C) * r * sp_lam[:, sl]
        a = jnp.exp(log_a)
        bv = jnp.sqrt(-jnp.tanh(log_a) * (a * a + 1.0)) * (ig * xg)
        for rg in range(rs // 8):
            for cl in range(gl):
                dst = pl.ds((s * rs + 8 * rg) * nl + g * gl + cl, 8, stride=nl)
                fa_ref[dst, :] = a[8 * rg:8 * rg + 8, cl * LANE:(cl + 1) * LANE]
                fb_ref[dst, :] = bv[8 * rg:8 * rg + 8, cl * LANE:(cl + 1) * LANE]

    def keys(s):
        kn = _head_rmsnorm(proj(3, s), kg_ref[...])
        for h in range(N_HEADS):
            k_ref[0, rsl(s), h * HEAD_DIM:(h + 1) * HEAD_DIM] = kn[h]

    def values(s):
        v_ref[0, rsl(s), :] = proj(4, s)

    def queries(s):
        qn = _head_rmsnorm(proj(2, s), qg_ref[...])
        for h in range(N_HEADS):
            q_ref[0, rsl(s), h * HEAD_DIM:(h + 1) * HEAD_DIM] = qn[h].astype(BF16)

    def gelu_gate(s):
        gels[s] = _gelu_tanh(proj(1, s))

    wide = [f for s in subs for f in (keys, values, queries)] + [gelu_gate] * nsub
    wide_args = [s for s in subs for _ in range(3)] + list(subs)
    small = [(s, g) for s in subs for g in range(ngrp)]
    conv(0)
    for n in range(max(len(wide), len(small))):
        if n < len(wide):
            wide[n](wide_args[n])
        if n < len(small):
            s, g = small[n]
            if g == ngrp - 1 and s + 1 < nsub:
                conv(s + 1)
            gate(s, g)
    tail = xp_ref[0, pad + tm - (kw - 1):pad + tm, :]
    tail_ref[0] = tail
    xp_ref[0, pad - (kw - 1):pad, :] = tail

    g_rnns = [_sigmoid(proj(5, s)) for s in subs]
    for s in subs:
        sg_ref[0, rsl(s), :] = _sigmoid(proj(6, s)).astype(BF16)

    h = hc_ref[...]
    for t in range(tm):
        h = fa_ref[t * nl:(t + 1) * nl, :] * h + fb_ref[t * nl:(t + 1) * nl, :]
        fb_ref[t * nl:(t + 1) * nl, :] = h
    hc_ref[...] = h
    for c in range(nl):
        hl_ref[0, :, c * LANE:(c + 1) * LANE] = h[c:c + 1, :]

    for s in subs:
        hseq = jnp.concatenate(
            [jnp.concatenate([fb_ref[pl.ds((s * rs + 8 * rg) * nl + c, 8, stride=nl), :] for c in range(nl)], axis=1)
             for rg in range(rs // 8)], axis=0)
        y_rnn = (hseq * gels[s]).astype(BF16)
        m = g_rnns[s] * jnp.dot(y_rnn, wpr_ref[...], preferred_element_type=F32)
        m_ref[0, rsl(s), :] = m.astype(BF16)


def _front(x, hist, h0, ln1, w_in, cw, cb, wg, ba, bx, lam, qg, kg, wpr, *, bb, tm, nsub):
    nb, t, d = x.shape
    kw = cw.shape[0]
    grid = (nb // bb, t // tm)
    tile = lambda: pl.BlockSpec((bb, tm, d), lambda b, i: (b, i, 0))
    full = lambda a: pl.BlockSpec(a.shape, lambda b, i: (0,) * a.ndim)
    per_b = lambda n: pl.BlockSpec((bb, n, d), lambda b, i: (b, 0, 0))
    act = lambda dt: jax.ShapeDtypeStruct((nb, t, d), dt)
    if bb == 1:
        body = functools.partial(_front_chain_kernel, tm=tm, d=d, nsub=nsub)
        scratch = [pltpu.VMEM((tm, d), BF16), pltpu.VMEM((1, 8 + tm, d), F32),
                   pltpu.VMEM((tm * d // LANE, LANE), F32), pltpu.VMEM((tm * d // LANE, LANE), F32),
                   pltpu.VMEM((d // LANE, LANE), F32), pltpu.VMEM((tm, d), F32)]
    else:
        body = functools.partial(_front_kernel, bb=bb, tm=tm, d=d)
        scratch = [pltpu.VMEM((bb * tm, d), BF16), pltpu.VMEM((bb, 8 + tm, d), F32),
                   pltpu.VMEM((bb * tm, d), F32), pltpu.VMEM((bb * tm, d), F32),
                   pltpu.VMEM((bb, 1, d), F32)]
    return pl.pallas_call(
        body,
        grid=grid,
        in_specs=[tile(), per_b(kw - 1), per_b(1)] + [full(a) for a in (ln1, w_in, cw, cb, wg, ba, bx, lam, qg, kg, wpr)],
        out_specs=[tile(), tile(), tile(), tile(), tile(), per_b(kw - 1), per_b(1)],
        out_shape=[act(BF16), act(F32), act(F32), act(BF16), act(BF16),
                   jax.ShapeDtypeStruct((nb, kw - 1, d), F32), jax.ShapeDtypeStruct((nb, 1, d), F32)],
        scratch_shapes=scratch,
        compiler_params=pltpu.CompilerParams(dimension_semantics=("arbitrary", "arbitrary"),
                                             vmem_limit_bytes=VMEM_LIMIT_BYTES),
        name="front",
    )(x, hist, h0, ln1, w_in, cw, cb, wg, ba, bx, lam, qg, kg, wpr)


def _attn_kernel(*refs, bq, bk, nq, heads, has_cache):
    if has_cache:
        q_ref, kn_ref, vn_ref, kp_ref, vp_ref, u_ref, o_ref, acc_ref, r_ref, alive_ref, kpad_ref, vpad_ref = refs
    else:
        q_ref, kn_ref, vn_ref, u_ref, o_ref, acc_ref, r_ref, alive_ref = refs
        kp_ref, vp_ref = kn_ref, vn_ref
    scale = HEAD_DIM ** -0.5
    rh = min(bq, LANE)
    ngrp = bq // rh
    chains = [(h, g) for h in range(heads) for g in range(ngrp)]

    def head_rows(ref, start, n, h):
        if ref.shape[-1] == HEAD_DIM and heads > 1:
            return ref[0, pl.ds(start * N_HEADS + h, n, stride=N_HEADS), :]
        lanes = slice(h * HEAD_DIM, (h + 1) * HEAD_DIM)
        return ref[pl.ds(start, n), lanes] if len(ref.shape) == 2 else ref[0, pl.ds(start, n), lanes]

    def q_block(qi):
        i = pl.program_id(2) * nq + qi
        qoff = 0 if nq == 1 else pl.multiple_of(qi * bq, bq)

        def run(tiles, state):
            zs = []
            for c, k_ref, _, start, nk, _ in tiles:
                h, g = chains[c]
                qh = q_ref[0, pl.ds(qoff + g * rh, rh), h * HEAD_DIM:(h + 1) * HEAD_DIM]
                zs.append(lax.dot_general(qh, head_rows(k_ref, start, nk, h).astype(BF16), (((1,), (1,)), ((), ())),
                                          preferred_element_type=F32) * scale)
            log_betas, log_1ms = [], []
            for (_, _, _, _, _, valid), z in zip(tiles, zs):
                log_beta = jnp.minimum(z, 0.0) - jnp.log(1.0 + jnp.exp(-jnp.abs(z)))
                log_1m = log_beta - z
                log_betas.append(log_beta)
                log_1ms.append(log_1m if valid is None else jnp.where(valid, log_1m, 0.0))
            suffixes = [jnp.dot(l.astype(BF16), u_ref[0:t[4], 0:t[4]], preferred_element_type=F32)
                        for t, l in zip(tiles, log_1ms)]
            state = list(state)
            ws = []
            for n, (c, _, _, _, nk, valid) in enumerate(tiles):
                arg = log_betas[n] + suffixes[n]
                rs = jnp.broadcast_to(jnp.sum(log_1ms[n], axis=-1, keepdims=True), (rh, LANE))
                if state[c] is None:
                    state[c] = (rs, None)
                else:
                    r, acc = state[c]
                    arg = arg + (r if nk == LANE else jnp.concatenate([r] * (nk // LANE), axis=1))
                    state[c] = (r + rs, acc)
                w = jnp.exp(arg)
                ws.append((w if valid is None else jnp.where(valid, w, 0.0)).astype(BF16))
            for n, (c, _, v_ref, start, nk, _) in enumerate(tiles):
                pv = jnp.dot(ws[n], head_rows(v_ref, start, nk, chains[c][0]).astype(BF16),
                             preferred_element_type=F32)
                r, acc = state[c]
                state[c] = (r, pv if acc is None else acc + pv)
            return state

        def save(state):
            for c, entry in enumerate(state):
                if entry is not None:
                    r_ref[c], acc_ref[c] = entry

        def load(only=None):
            return [(r_ref[c], acc_ref[c]) if only is None or only(g) else None for c, (_, g) in enumerate(chains)]

        def live():
            return jnp.max(r_ref[...]) > SUFFIX_FLOOR

        def first(tiles):
            state = run(tiles, [None] * len(chains))
            save(state)
            top = functools.reduce(jnp.maximum, [r for r, _ in state])
            alive_ref[0] = (jnp.max(top) > SUFFIX_FLOOR).astype(jnp.int32)

        if has_cache:
            tn = kn_ref.shape[1]
            for pad_ref, src_ref in ((kpad_ref, kn_ref), (vpad_ref, vn_ref)):
                pad_ref[...] = jnp.zeros_like(pad_ref)
                pad_ref[0:tn, :] = src_ref[0]
            kd_ref, vd_ref, dstart = kpad_ref, vpad_ref, 0
        else:
            kd_ref, vd_ref, dstart = kn_ref, vn_ref, pl.multiple_of(i * bq, bq)
        diag = []
        for c, (h, g) in enumerate(chains):
            nk = -(-((g + 1) * rh) // LANE) * LANE
            row = lax.broadcasted_iota(jnp.int32, (rh, nk), 0) + g * rh
            col = lax.broadcasted_iota(jnp.int32, (rh, nk), 1)
            diag.append((c, kd_ref, vd_ref, dstart, nk, col < row))

        def past(j):
            start = j * bk if isinstance(j, int) else pl.multiple_of(j * bk, bk)
            return [(c, kp_ref, vp_ref, start, bk, None) for c in range(len(chains))]

        if has_cache:
            j0 = kp_ref.shape[1] // N_HEADS // bk - 1
            first(diag + past(j0))
            j1 = jnp.int32(j0 - 1)
        else:
            def earlier(lo, n):
                return pl.multiple_of(dstart - bk + lo, LANE), n

            @pl.when(i > 0)
            def _():
                first(diag + [(c, kp_ref, vp_ref) + earlier(g * rh, bk - g * rh) + (None,)
                              for c, (_, g) in enumerate(chains)])

            @pl.when(i == 0)
            def _():
                first(diag)
                alive_ref[0] = 0
            j1 = i - 2

        @pl.when(alive_ref[0] > 0)
        def _():
            if not has_cache and ngrp > 1:
                save(run([(c, kp_ref, vp_ref) + earlier(0, g * rh) + (None,)
                          for c, (_, g) in enumerate(chains) if g > 0], load(lambda g: g > 0)))

            def cond(c):
                return jnp.logical_and(c[0] >= 0, c[1])

            def body(c):
                save(run(past(c[0]), load()))
                return c[0] - 1, live()
            lax.while_loop(cond, body, (j1, live()))

        for c, (h, g) in enumerate(chains):
            o_ref[0, pl.ds(qoff + g * rh, rh), h * HEAD_DIM:(h + 1) * HEAD_DIM] = acc_ref[c].astype(o_ref.dtype)

    if nq == 1:
        q_block(0)
    else:
        def q_step(qi, carry):
            q_block(qi)
            return carry
        lax.fori_loop(0, nq, q_step, 0)


def _attention(q, kn, vn, kc=None, vc=None, *, bq, bk, nq, heads):
    nb, t, d = q.shape
    hw = heads * HEAD_DIM
    has_cache = kc is not None
    assert has_cache or (bk == bq and bq % LANE == 0)
    assert not has_cache or (heads == N_HEADS and kc.shape[2] == HEAD_DIM and t == bq <= LANE and nq == 1)
    rh = min(bq, LANE)
    nchain = heads * (bq // rh)
    grid = (nb, d // hw, t // (bq * nq))
    qspec = pl.BlockSpec((1, bq * nq, hw), lambda b, g, i: (b, i, g))
    kvspec = lambda a: pl.BlockSpec((1, a.shape[1], hw), lambda b, g, i: (b, 0, g))
    cachespec = lambda a: pl.BlockSpec((1,) + a.shape[1:], lambda b, g, i: (b, 0, 0))
    u = jnp.tril(jnp.ones((bk, bk), F32), -1).astype(BF16)
    kv = [kn, vn] + ([kc, vc] if has_cache else [])
    scratch = [pltpu.VMEM((nchain, rh, HEAD_DIM), F32), pltpu.VMEM((nchain, rh, LANE), F32),
               pltpu.SMEM((1,), jnp.int32)]
    if has_cache:
        scratch += [pltpu.VMEM((LANE, hw), F32), pltpu.VMEM((LANE, hw), F32)]
    return pl.pallas_call(
        functools.partial(_attn_kernel, bq=bq, bk=bk, nq=nq, heads=heads, has_cache=has_cache),
        grid=grid,
        in_specs=[qspec, kvspec(kn), kvspec(vn)] + [cachespec(a) for a in kv[2:]]
        + [pl.BlockSpec(u.shape, lambda b, g, i: (0, 0))],
        out_specs=qspec,
        out_shape=jax.ShapeDtypeStruct((nb, t, d), BF16),
        scratch_shapes=scratch,
        compiler_params=pltpu.CompilerParams(dimension_semantics=("arbitrary", "arbitrary", "arbitrary"),
                                             vmem_limit_bytes=VMEM_LIMIT_BYTES),
        name="attn_cache" if has_cache else "attn",
    )(q, *kv, u)


def _back_kernel(x_ref, m_ref, sg_ref, ya_ref, fh_ref, wpa_ref, wo_ref, ln2_ref, wup_ref, fw_ref, fb_ref, wdn_ref,
                 y_ref, ft_ref, gp_ref, u2_ref, val_ref, *, bb, tm, d, dff, nsub):
    i = pl.program_id(1)
    rows = bb * tm
    rs = rows // nsub
    kw = fw_ref.shape[0]
    pad = 8
    nchunk = dff // d
    subs = range(nsub)

    def load(ref, s):
        return ref[0, s * rs:(s + 1) * rs, :] if bb == 1 else ref[...].reshape(rows, d)

    def rsl(s):
        return slice(s * rs, (s + 1) * rs)

    @pl.when(i == 0)
    def _():
        for c in range(nchunk):
            gp_ref[c, :, pad - (kw - 1):pad, :] = fh_ref[:, :, c * d:(c + 1) * d]

    atts = [jnp.dot(load(ya_ref, s), wpa_ref[...], preferred_element_type=F32) for s in subs]
    merged = [(load(m_ref, s).astype(F32) + load(sg_ref, s).astype(F32) * atts[s]).astype(BF16) for s in subs]
    outs = [jnp.dot(merged[s], wo_ref[...], preferred_element_type=F32) for s in subs]
    for s in subs:
        x1 = load(x_ref, s) + outs[s]
        if bb == 1:
            y_ref[0, rsl(s), :] = x1
        else:
            y_ref[...] = x1.reshape(bb, tm, d)
        u2_ref[rsl(s), :] = ((x1 * _rms_scale(x1)) * ln2_ref[...]).astype(BF16)

    def up(c):
        for s in subs:
            gp = jnp.dot(u2_ref[rsl(s), :], wup_ref[:, c * d:(c + 1) * d], preferred_element_type=F32)
            if bb == 1:
                gp_ref[c, 0, pad + s * rs:pad + (s + 1) * rs, :] = gp
            else:
                for b in range(bb):
                    gp_ref[c, b, pad:pad + tm, :] = gp[b * tm:(b + 1) * tm]
            val_ref[c % 2, rsl(s), :] = jnp.dot(u2_ref[rsl(s), :], wup_ref[:, dff + c * d:dff + (c + 1) * d],
                                                preferred_element_type=F32)

    def conv(c, b, start, n):
        cs = slice(c * d, (c + 1) * d)
        off = start - (kw - 1)
        g = gp_ref[c, b, off:off + n, :] * fw_ref[0:1, cs]
        for j in range(1, kw):
            g = g + gp_ref[c, b, off + j:off + j + n, :] * fw_ref[j:j + 1, cs]
        return g + fb_ref[:, cs]

    up(0)
    for c in range(nchunk):
        if c + 1 < nchunk:
            up(c + 1)
        cs = slice(c * d, (c + 1) * d)
        for s in subs:
            if bb == 1:
                gc = conv(c, 0, pad + s * rs, rs)
            else:
                gc = jnp.concatenate([conv(c, b, pad, tm) for b in range(bb)], axis=0)
            act = (_gelu_tanh(gc) * val_ref[c % 2, rsl(s), :]).astype(BF16)
            part = jnp.dot(act, wdn_ref[c * d:(c + 1) * d, :], preferred_element_type=F32)
            if bb == 1:
                y_ref[0, rsl(s), :] += part
            else:
                y_ref[...] += part.reshape(bb, tm, d)
        for b in range(bb):
            tail = gp_ref[c, b, pad + tm - (kw - 1):pad + tm, :]
            ft_ref[b, :, cs] = tail
            gp_ref[c, b, pad - (kw - 1):pad, :] = tail


def _back(x, m, sg, ya, fh, wpa, wo, ln2, wup, fw, fb, wdn, *, bb, tm, nsub):
    nb, t, d = x.shape
    dff = wdn.shape[0]
    kw = fw.shape[0]
    assert bb == 1 or nsub == 1
    grid = (nb // bb, t // tm)
    tile = lambda: pl.BlockSpec((bb, tm, d), lambda b, i: (b, i, 0))
    full = lambda a: pl.BlockSpec(a.shape, lambda b, i: (0,) * a.ndim)
    hist = pl.BlockSpec((bb, kw - 1, dff), lambda b, i: (b, 0, 0))
    return pl.pallas_call(
        functools.partial(_back_kernel, bb=bb, tm=tm, d=d, dff=dff, nsub=nsub),
        grid=grid,
        in_specs=[tile(), tile(), tile(), tile(), hist] + [full(a) for a in (wpa, wo, ln2, wup, fw, fb, wdn)],
        out_specs=[tile(), hist],
        out_shape=[jax.ShapeDtypeStruct((nb, t, d), F32), jax.ShapeDtypeStruct((nb, kw - 1, dff), F32)],
        scratch_shapes=[pltpu.VMEM((dff // d, bb, 8 + tm, d), F32), pltpu.VMEM((bb * tm, d), BF16),
                        pltpu.VMEM((2, bb * tm, d), F32)],
        compiler_params=pltpu.CompilerParams(dimension_semantics=("arbitrary", "arbitrary"),
                                             vmem_limit_bytes=VMEM_LIMIT_BYTES),
        name="back",
    )(x, m, sg, ya, fh, wpa, wo, ln2, wup, fw, fb, wdn)


def _dense_gate_blocks(w):
    nblk, c, _ = w.shape
    per = GATE_GROUP // c
    w4 = w.reshape(nblk // per, per, c, c)
    eye = jnp.eye(per, dtype=w.dtype)
    return jnp.einsum('gncd,nm->gncmd', w4, eye).reshape(nblk // per, GATE_GROUP, GATE_GROUP)


def _layer(x, hist, h0, fh, kc, vc, wts, *, bb, tm, tmb, nsub, bq, bk, nq, heads):
    (ln1, w_in, cw, cb, wg, ba, bx, lam, qg, kg, wpr, wpa, wo, ln2, wup, fw, fb, wdn) = wts
    q, k, v, m, sg, tail, hl = _front(x, hist, h0, ln1, w_in, cw, cb, wg, ba, bx, lam, qg, kg, wpr,
                                      bb=bb, tm=tm, nsub=nsub)
    ya = _attention(q, k, v, kc, vc, bq=bq, bk=bk, nq=nq, heads=heads)
    y, ft = _back(x, m, sg, ya, fh, wpa, wo, ln2, wup, fw, fb, wdn, bb=bb, tm=tmb, nsub=nsub)
    return y, k, v, tail, hl, ft


def kernel(x_prompt, x_sample, cache_k, cache_v, state_rnn_conv, state_rnn_h, state_ffn_conv, ln1, w_in, rnn_conv_w, rnn_conv_b, lru_wa, lru_ba, lru_wx, lru_bx, lru_lambda, q_norm_g, k_norm_g, w_proj_rnn, w_proj_attn, w_out, ln2, w_up, ffn_conv_w, ffn_conv_b, w_down):
    depth = ln1.shape[0]
    assert depth == 1
    bp, tp, d = x_prompt.shape
    bs, ts, _ = x_sample.shape
    dff = w_down.shape[1]
    row = lambda a: a[0].reshape(1, -1)
    wg = jnp.concatenate([_dense_gate_blocks(lru_wa[0]), _dense_gate_blocks(lru_wx[0])], axis=-1).astype(BF16)
    wts = (row(ln1), w_in[0].astype(BF16), rnn_conv_w[0], row(rnn_conv_b), wg, row(lru_ba), row(lru_bx),
           row(lru_lambda), row(q_norm_g), row(k_norm_g), w_proj_rnn[0].astype(BF16),
           w_proj_attn[0].astype(BF16), w_out[0].astype(BF16), row(ln2), w_up[0].astype(BF16),
           ffn_conv_w[0], row(ffn_conv_b), w_down[0].astype(BF16))
    kw = rnn_conv_w.shape[1]
    fkw = ffn_conv_w.shape[1]

    zc = jnp.zeros((bp, kw - 1, d), F32)
    zh = jnp.zeros((bp, 1, d), F32)
    zf = jnp.zeros((bp, fkw - 1, w_down.shape[1]), F32)
    yp, kp, vp, rcp, hp, fcp = _layer(x_prompt, zc, zh, zf, None, None, wts,
                                      bb=1, tm=512, tmb=512, nsub=2, bq=256, bk=256, nq=4, heads=4)
    ys, ks, vs, rcs, hs, fcs = _layer(x_sample, state_rnn_conv[0], state_rnn_h[0].reshape(bs, 1, d),
                                      state_ffn_conv[0], cache_k[0].reshape(bs, -1, HEAD_DIM),
                                      cache_v[0].reshape(bs, -1, HEAD_DIM), wts,
                                      bb=bs, tm=ts, tmb=ts, nsub=1, bq=ts, bk=256, nq=1, heads=N_HEADS)
    hd = lambda a: a.reshape(1, a.shape[0], a.shape[1], N_HEADS, HEAD_DIM)
    return (yp, ys, hd(kp), hd(vp), rcp[None], hp.reshape(1, bp, d), fcp[None],
            hd(ks), hd(vs), rcs[None], hs.reshape(1, bs, d), fcs[None])
```

```python
import functools
import math

import jax
import jax.numpy as jnp
from jax import lax
from jax.experimental import pallas as pl
from jax.experimental.pallas import tpu as pltpu

F32 = jnp.float32
BF16 = jnp.bfloat16

NORM_EPS = 1e-6
LRU_C = 8.0
N_HEADS = 8
HEAD_DIM = 128
N_LRU_BLOCKS = 16
LANE = 128
GATE_GROUP = 256
VMEM_LIMIT_BYTES = 56 * 1024 * 1024
SUFFIX_FLOOR = -104.0


def _gelu_tanh(x):
    return 0.5 * x * (1.0 + jnp.tanh(math.sqrt(2.0 / math.pi) * (x + 0.044715 * (x * x * x))))


def _sigmoid(x):
    return 0.5 * jnp.tanh(0.5 * x) + 0.5


def _when(cond):
    if isinstance(cond, bool):
        return (lambda f: f()) if cond else (lambda f: None)
    return pl.when(cond)


def _softplus(x):
    return jnp.maximum(x, 0.0) + jnp.log1p(jnp.exp(-jnp.abs(x)))


def _rms_scale(x):
    return lax.rsqrt(jnp.mean(x * x, axis=-1, keepdims=True) + NORM_EPS)


def _head_rmsnorm(x, g):
    outs = []
    for h in range(N_HEADS):
        xh = x[:, h * HEAD_DIM:(h + 1) * HEAD_DIM]
        outs.append(xh * _rms_scale(xh) * g)
    return outs


def _front_kernel(x_ref, hist_ref, h0_ref, ln1_ref, w_in_ref, cw_ref, cb_ref, wg_ref, ba_ref, bx_ref,
                  lam_ref, qg_ref, kg_ref, wpr_ref,
                  q_ref, k_ref, v_ref, m_ref, sg_ref, tail_ref, hl_ref,
                  u_ref, xp_ref, a_ref, b_ref, hc_ref, *, bb, tm, d, first):
    rows = bb * tm
    kw = cw_ref.shape[0]
    pad = 8

    @_when(first)
    def _():
        xp_ref[:, pad - (kw - 1):pad, :] = hist_ref[...]
        hc_ref[...] = h0_ref[...]

    x = x_ref[...].reshape(rows, d)
    u_ref[...] = ((x * _rms_scale(x)) * ln1_ref[...]).astype(BF16)

    def proj(c):
        return jnp.dot(u_ref[...], w_in_ref[:, c * d:(c + 1) * d], preferred_element_type=F32)

    xr = proj(0)
    for b in range(bb):
        xp_ref[b, pad:pad + tm, :] = xr[b * tm:(b + 1) * tm]
    xcs = []
    for b in range(bb):
        off = pad - (kw - 1)
        y = xp_ref[b, off:off + tm, :] * cw_ref[0:1, :]
        for j in range(1, kw):
            y = y + xp_ref[b, off + j:off + j + tm, :] * cw_ref[j:j + 1, :]
        xcs.append(y + cb_ref[...])
        tail = xp_ref[b, pad + tm - (kw - 1):pad + tm, :]
        tail_ref[b] = tail
        xp_ref[b, pad - (kw - 1):pad, :] = tail
    xc = xcs[0] if bb == 1 else jnp.concatenate(xcs, axis=0)

    sp_lam = _softplus(-lam_ref[...])
    for g in range(d // GATE_GROUP):
        sl = slice(g * GATE_GROUP, (g + 1) * GATE_GROUP)
        xg = xc[:, sl]
        pre = jnp.dot(xg.astype(BF16), wg_ref[g], preferred_element_type=F32)
        r = _sigmoid(pre[:, :GATE_GROUP] + ba_ref[:, sl])
        ig = _sigmoid(pre[:, GATE_GROUP:] + bx_ref[:, sl])
        log_a = (-LRU_C) * r * sp_lam[:, sl]
        a = jnp.exp(log_a)
        one_m_a2 = -jnp.tanh(log_a) * (a * a + 1.0)
        a_ref[:, sl] = a
        b_ref[:, sl] = jnp.sqrt(one_m_a2) * (ig * xg)

    for b in range(bb):
        def step(t, h, b=b):
            r = b * tm + t
            h = a_ref[pl.ds(r, 1), :] * h + b_ref[pl.ds(r, 1), :]
            b_ref[pl.ds(r, 1), :] = h
            return h
        h_end = lax.fori_loop(0, tm, step, hc_ref[b], unroll=8)
        hc_ref[b] = h_end
        hl_ref[b] = h_end

    y_rnn = (b_ref[...] * _gelu_tanh(proj(1))).astype(BF16)
    m = _sigmoid(proj(5)) * jnp.dot(y_rnn, wpr_ref[...], preferred_element_type=F32)
    m_ref[...] = m.astype(BF16).reshape(bb, tm, d)

    qn = _head_rmsnorm(proj(2), qg_ref[...])
    for h in range(N_HEADS):
        q_ref[:, :, h * HEAD_DIM:(h + 1) * HEAD_DIM] = qn[h].astype(BF16).reshape(bb, tm, HEAD_DIM)
    kn = _head_rmsnorm(proj(3), kg_ref[...])
    for h in range(N_HEADS):
        k_ref[:, :, h * HEAD_DIM:(h + 1) * HEAD_DIM] = kn[h].reshape(bb, tm, HEAD_DIM)
    v_ref[...] = proj(4).reshape(bb, tm, d)
    sg_ref[...] = _sigmoid(proj(6)).astype(BF16).reshape(bb, tm, d)


def _front_chain_kernel(x_ref, hist_ref, h0_ref, ln1_ref, w_in_ref, cw_ref, cb_ref, wg_ref, ba_ref, bx_ref,
                        lam_ref, qg_ref, kg_ref, wpr_ref,
                        q_ref, k_ref, v_ref, m_ref, sg_ref, tail_ref, hl_ref,
                        u_ref, xp_ref, fa_ref, fb_ref, hc_ref, xc_ref, *, tm, d, nsub, first):
    rs = tm // nsub
    kw = cw_ref.shape[0]
    pad = 8
    nl = d // LANE
    gl = GATE_GROUP // LANE
    subs = range(nsub)

    def rsl(s):
        return slice(s * rs, (s + 1) * rs)

    @_when(first)
    def _():
        xp_ref[0, pad - (kw - 1):pad, :] = hist_ref[0]
        for c in range(nl):
            hc_ref[c:c + 1, :] = h0_ref[0, :, c * LANE:(c + 1) * LANE]

    for s in subs:
        x = x_ref[0, rsl(s), :]
        u_ref[rsl(s), :] = ((x * _rms_scale(x)) * ln1_ref[...]).astype(BF16)

    def proj(c, s):
        return jnp.dot(u_ref[rsl(s), :], w_in_ref[:, c * d:(c + 1) * d], preferred_element_type=F32)

    for s in subs:
        xp_ref[0, pad + s * rs:pad + (s + 1) * rs, :] = proj(0, s)

    ngrp = d // GATE_GROUP
    sp_lam = _softplus(-lam_ref[...])
    gels = [None] * nsub

    def conv(s):
        off = pad + s * rs - (kw - 1)
        xc = xp_ref[0, off:off + rs, :] * cw_ref[0:1, :]
        for j in range(1, kw):
            xc = xc + xp_ref[0, off + j:off + j + rs, :] * cw_ref[j:j + 1, :]
        xc_ref[rsl(s), :] = xc + cb_ref[...]

    def gate(s, g):
        sl = slice(g * GATE_GROUP, (g + 1) * GATE_GROUP)
        xg = xc_ref[rsl(s), sl]
        pre = jnp.dot(xg.astype(BF16), wg_ref[g], preferred_element_type=F32)
        r = _sigmoid(pre[:, :GATE_GROUP] + ba_ref[:, sl])
        ig = _sigmoid(pre[:, GATE_GROUP:] + bx_ref[:, sl])
        log_a = (-LRU_C) * r * sp_lam[:, sl]
        a = jnp.exp(log_a)
        bv = jnp.sqrt(-jnp.tanh(log_a) * (a * a + 1.0)) * (ig * xg)
        for rg in range(rs // 8):
            for cl in range(gl):
                dst = pl.ds((s * rs + 8 * rg) * nl + g * gl + cl, 8, stride=nl)
                fa_ref[dst, :] = a[8 * rg:8 * rg + 8, cl * LANE:(cl + 1) * LANE]
                fb_ref[dst, :] = bv[8 * rg:8 * rg + 8, cl * LANE:(cl + 1) * LANE]

    def keys(s):
        kn = _head_rmsnorm(proj(3, s), kg_ref[...])
        for h in range(N_HEADS):
            k_ref[0, rsl(s), h * HEAD_DIM:(h + 1) * HEAD_DIM] = kn[h]

    def values(s):
        v_ref[0, rsl(s), :] = proj(4, s)

    def queries(s):
        qn = _head_rmsnorm(proj(2, s), qg_ref[...])
        for h in range(N_HEADS):
            q_ref[0, rsl(s), h * HEAD_DIM:(h + 1) * HEAD_DIM] = qn[h].astype(BF16)

    def gelu_gate(s):
        gels[s] = _gelu_tanh(proj(1, s))

    wide = [f for s in subs for f in (keys, values, queries)] + [gelu_gate] * nsub
    wide_args = [s for s in subs for _ in range(3)] + list(subs)
    small = [(s, g) for s in subs for g in range(ngrp)]
    conv(0)
    for n in range(max(len(wide), len(small))):
        if n < len(wide):
            wide[n](wide_args[n])
        if n < len(small):
            s, g = small[n]
            if g == ngrp - 1 and s + 1 < nsub:
                conv(s + 1)
            gate(s, g)
    tail = xp_ref[0, pad + tm - (kw - 1):pad + tm, :]
    tail_ref[0] = tail
    xp_ref[0, pad - (kw - 1):pad, :] = tail

    g_rnns = [_sigmoid(proj(5, s)) for s in subs]
    for s in subs:
        sg_ref[0, rsl(s), :] = _sigmoid(proj(6, s)).astype(BF16)

    h = hc_ref[...]
    for t in range(tm):
        h = fa_ref[t * nl:(t + 1) * nl, :] * h + fb_ref[t * nl:(t + 1) * nl, :]
        fb_ref[t * nl:(t + 1) * nl, :] = h
    hc_ref[...] = h
    for c in range(nl):
        hl_ref[0, :, c * LANE:(c + 1) * LANE] = h[c:c + 1, :]

    for s in subs:
        hseq = jnp.concatenate(
            [jnp.concatenate([fb_ref[pl.ds((s * rs + 8 * rg) * nl + c, 8, stride=nl), :] for c in range(nl)], axis=1)
             for rg in range(rs // 8)], axis=0)
        y_rnn = (hseq * gels[s]).astype(BF16)
        m = g_rnns[s] * jnp.dot(y_rnn, wpr_ref[...], preferred_element_type=F32)
        m_ref[0, rsl(s), :] = m.astype(BF16)


def _stream_specs(nb, t, tm, d):
    nt = t // tm
    nprompt = nb * nt

    def where(s):
        c = jnp.minimum(s, nprompt - 1)
        return c // nt, c % nt
    tile = pl.BlockSpec((1, tm, d), lambda s: (*where(s), 0))
    stream = lambda n, w: pl.BlockSpec((1, n, w), lambda s: (where(s)[0], 0, 0))
    whole = lambda shape: pl.BlockSpec(tuple(shape), lambda s: (0,) * len(shape))
    return nt, nprompt, tile, stream, whole


def _front(xp, hist_p, h0_p, xs, hist_s, h0_s, weights, *, tm, nsub):
    nb, t, d = xp.shape
    bs, ts, _ = xs.shape
    kw = hist_p.shape[1] + 1
    nt, nprompt, tile, stream, whole = _stream_specs(nb, t, tm, d)
    ins = (xp, hist_p, h0_p, xs, hist_s, h0_s) + tuple(weights)

    def body(*refs):
        ins_p, ins_s, w = refs[0:3], refs[3:6], refs[6:6 + len(weights)]
        o = 6 + len(weights)
        outs_p, outs_s, scr_p, scr_s = refs[o:o + 7], refs[o + 7:o + 14], refs[o + 14:o + 20], refs[o + 20:o + 25]
        s = pl.program_id(0)

        @pl.when(s < nprompt)
        def _():
            _front_chain_kernel(*ins_p, *w, *outs_p, *scr_p, tm=tm, d=d, nsub=nsub, first=s % nt == 0)

        @pl.when(s == nprompt)
        def _():
            _front_kernel(*ins_s, *w, *outs_s, *scr_s, bb=bs, tm=ts, d=d, first=True)

    def outs(n, rows):
        act = lambda dt: jax.ShapeDtypeStruct((n, rows, d), dt)
        return [act(BF16), act(F32), act(F32), act(BF16), act(BF16),
                jax.ShapeDtypeStruct((n, kw - 1, d), F32), jax.ShapeDtypeStruct((n, 1, d), F32)]
    out_shape = outs(nb, t) + outs(bs, ts)
    return pl.pallas_call(
        body,
        grid=(nprompt + 1,),
        in_specs=[tile, stream(kw - 1, d), stream(1, d)] + [whole(a.shape) for a in ins[3:]],
        out_specs=[tile] * 5 + [stream(kw - 1, d), stream(1, d)] + [whole(o.shape) for o in out_shape[7:]],
        out_shape=out_shape,
        scratch_shapes=[pltpu.VMEM((tm, d), BF16), pltpu.VMEM((1, 8 + tm, d), F32),
                        pltpu.VMEM((tm * d // LANE, LANE), F32), pltpu.VMEM((tm * d // LANE, LANE), F32),
                        pltpu.VMEM((d // LANE, LANE), F32), pltpu.VMEM((tm, d), F32),
                        pltpu.VMEM((bs * ts, d), BF16), pltpu.VMEM((bs, 8 + ts, d), F32),
                        pltpu.VMEM((bs * ts, d), F32), pltpu.VMEM((bs * ts, d), F32), pltpu.VMEM((bs, 1, d), F32)],
        compiler_params=pltpu.CompilerParams(dimension_semantics=("arbitrary",),
                                             vmem_limit_bytes=VMEM_LIMIT_BYTES),
        name="front",
    )(*ins)


def _attn_kernel(*refs, bq, bk, nq, heads, has_cache):
    if has_cache:
        q_ref, kn_ref, vn_ref, kp_ref, vp_ref, u_ref, o_ref, acc_ref, r_ref, alive_ref, kpad_ref, vpad_ref = refs
    else:
        q_ref, kn_ref, vn_ref, u_ref, o_ref, acc_ref, r_ref, alive_ref = refs
        kp_ref, vp_ref = kn_ref, vn_ref
    scale = HEAD_DIM ** -0.5
    rh = min(bq, LANE)
    ngrp = bq // rh
    chains = [(h, g) for h in range(heads) for g in range(ngrp)]

    def head_rows(ref, start, n, h):
        if ref.shape[-1] == HEAD_DIM and heads > 1:
            return ref[0, pl.ds(start * N_HEADS + h, n, stride=N_HEADS), :]
        lanes = slice(h * HEAD_DIM, (h + 1) * HEAD_DIM)
        return ref[pl.ds(start, n), lanes] if len(ref.shape) == 2 else ref[0, pl.ds(start, n), lanes]

    def q_block(qi):
        i = pl.program_id(2) * nq + qi
        qoff = 0 if nq == 1 else pl.multiple_of(qi * bq, bq)

        def run(tiles, state):
            zs = []
            for c, k_ref, _, start, nk, _ in tiles:
                h, g = chains[c]
                qh = q_ref[0, pl.ds(qoff + g * rh, rh), h * HEAD_DIM:(h + 1) * HEAD_DIM]
                zs.append(lax.dot_general(qh, head_rows(k_ref, start, nk, h).astype(BF16), (((1,), (1,)), ((), ())),
                                          preferred_element_type=F32) * scale)
            log_betas, log_1ms = [], []
            for (_, _, _, _, _, valid), z in zip(tiles, zs):
                log_beta = jnp.minimum(z, 0.0) - jnp.log(1.0 + jnp.exp(-jnp.abs(z)))
                log_1m = log_beta - z
                log_betas.append(log_beta)
                log_1ms.append(log_1m if valid is None else jnp.where(valid, log_1m, 0.0))
            suffixes = [jnp.dot(l.astype(BF16), u_ref[0:t[4], 0:t[4]], preferred_element_type=F32)
                        for t, l in zip(tiles, log_1ms)]
            state = list(state)
            ws = []
            for n, (c, _, _, _, nk, valid) in enumerate(tiles):
                arg = log_betas[n] + suffixes[n]
                rs = jnp.broadcast_to(jnp.sum(log_1ms[n], axis=-1, keepdims=True), (rh, LANE))
                if state[c] is None:
                    state[c] = (rs, None)
                else:
                    r, acc = state[c]
                    arg = arg + (r if nk == LANE else jnp.concatenate([r] * (nk // LANE), axis=1))
                    state[c] = (r + rs, acc)
                w = jnp.exp(arg)
                ws.append((w if valid is None else jnp.where(valid, w, 0.0)).astype(BF16))
            for n, (c, _, v_ref, start, nk, _) in enumerate(tiles):
                pv = jnp.dot(ws[n], head_rows(v_ref, start, nk, chains[c][0]).astype(BF16),
                             preferred_element_type=F32)
                r, acc = state[c]
                state[c] = (r, pv if acc is None else acc + pv)
            return state

        def save(state):
            for c, entry in enumerate(state):
                if entry is not None:
                    r_ref[c], acc_ref[c] = entry

        def load(only=None):
            return [(r_ref[c], acc_ref[c]) if only is None or only(g) else None for c, (_, g) in enumerate(chains)]

        def live():
            return jnp.max(r_ref[...]) > SUFFIX_FLOOR

        def first(tiles):
            state = run(tiles, [None] * len(chains))
            save(state)
            top = functools.reduce(jnp.maximum, [r for r, _ in state])
            alive_ref[0] = (jnp.max(top) > SUFFIX_FLOOR).astype(jnp.int32)

        if has_cache:
            tn = kn_ref.shape[1]
            for pad_ref, src_ref in ((kpad_ref, kn_ref), (vpad_ref, vn_ref)):
                pad_ref[...] = jnp.zeros_like(pad_ref)
                pad_ref[0:tn, :] = src_ref[0]
            kd_ref, vd_ref, dstart = kpad_ref, vpad_ref, 0
        else:
            kd_ref, vd_ref, dstart = kn_ref, vn_ref, pl.multiple_of(i * bq, bq)
        diag = []
        for c, (h, g) in enumerate(chains):
            nk = -(-((g + 1) * rh) // LANE) * LANE
            row = lax.broadcasted_iota(jnp.int32, (rh, nk), 0) + g * rh
            col = lax.broadcasted_iota(jnp.int32, (rh, nk), 1)
            diag.append((c, kd_ref, vd_ref, dstart, nk, col < row))

        def past(j):
            start = j * bk if isinstance(j, int) else pl.multiple_of(j * bk, bk)
            return [(c, kp_ref, vp_ref, start, bk, None) for c in range(len(chains))]

        if has_cache:
            j0 = kp_ref.shape[1] // N_HEADS // bk - 1
            first(diag + past(j0))
            j1 = jnp.int32(j0 - 1)
        else:
            def earlier(lo, n):
                return pl.multiple_of(dstart - bk + lo, LANE), n

            @pl.when(i > 0)
            def _():
                first(diag + [(c, kp_ref, vp_ref) + earlier(g * rh, bk - g * rh) + (None,)
                              for c, (_, g) in enumerate(chains)])

            @pl.when(i == 0)
            def _():
                first(diag)
                alive_ref[0] = 0
            j1 = i - 2

        @pl.when(alive_ref[0] > 0)
        def _():
            if not has_cache and ngrp > 1:
                save(run([(c, kp_ref, vp_ref) + earlier(0, g * rh) + (None,)
                          for c, (_, g) in enumerate(chains) if g > 0], load(lambda g: g > 0)))

            def cond(c):
                return jnp.logical_and(c[0] >= 0, c[1])

            def body(c):
                save(run(past(c[0]), load()))
                return c[0] - 1, live()
            lax.while_loop(cond, body, (j1, live()))

        for c, (h, g) in enumerate(chains):
            o_ref[0, pl.ds(qoff + g * rh, rh), h * HEAD_DIM:(h + 1) * HEAD_DIM] = acc_ref[c].astype(o_ref.dtype)

    if nq == 1:
        q_block(0)
    else:
        def q_step(qi, carry):
            q_block(qi)
            return carry
        lax.fori_loop(0, nq, q_step, 0)


def _attention(q, kn, vn, kc=None, vc=None, *, bq, bk, nq, heads):
    nb, t, d = q.shape
    hw = heads * HEAD_DIM
    has_cache = kc is not None
    assert has_cache or (bk == bq and bq % LANE == 0)
    assert not has_cache or (heads == N_HEADS and kc.shape[2] == HEAD_DIM and t == bq <= LANE and nq == 1)
    rh = min(bq, LANE)
    nchain = heads * (bq // rh)
    grid = (nb, d // hw, t // (bq * nq))
    qspec = pl.BlockSpec((1, bq * nq, hw), lambda b, g, i: (b, i, g))
    kvspec = lambda a: pl.BlockSpec((1, a.shape[1], hw), lambda b, g, i: (b, 0, g))
    cachespec = lambda a: pl.BlockSpec((1,) + a.shape[1:], lambda b, g, i: (b, 0, 0))
    u = jnp.tril(jnp.ones((bk, bk), F32), -1).astype(BF16)
    kv = [kn, vn] + ([kc, vc] if has_cache else [])
    scratch = [pltpu.VMEM((nchain, rh, HEAD_DIM), F32), pltpu.VMEM((nchain, rh, LANE), F32),
               pltpu.SMEM((1,), jnp.int32)]
    if has_cache:
        scratch += [pltpu.VMEM((LANE, hw), F32), pltpu.VMEM((LANE, hw), F32)]
    return pl.pallas_call(
        functools.partial(_attn_kernel, bq=bq, bk=bk, nq=nq, heads=heads, has_cache=has_cache),
        grid=grid,
        in_specs=[qspec, kvspec(kn), kvspec(vn)] + [cachespec(a) for a in kv[2:]]
        + [pl.BlockSpec(u.shape, lambda b, g, i: (0, 0))],
        out_specs=qspec,
        out_shape=jax.ShapeDtypeStruct((nb, t, d), BF16),
        scratch_shapes=scratch,
        compiler_params=pltpu.CompilerParams(dimension_semantics=("arbitrary", "arbitrary", "arbitrary"),
                                             vmem_limit_bytes=VMEM_LIMIT_BYTES),
        name="attn_cache" if has_cache else "attn",
    )(q, *kv, u)


def _back_kernel(x_ref, m_ref, sg_ref, ya_ref, fh_ref, wpa_ref, wo_ref, ln2_ref, wup_ref, fw_ref, fb_ref, wdn_ref,
                 y_ref, ft_ref, gp_ref, u2_ref, val_ref, *, bb, tm, d, dff, nsub, first):
    rows = bb * tm
    rs = rows // nsub
    kw = fw_ref.shape[0]
    pad = 8
    nchunk = dff // d
    subs = range(nsub)

    def load(ref, s):
        return ref[0, s * rs:(s + 1) * rs, :] if bb == 1 else ref[...].reshape(rows, d)

    def rsl(s):
        return slice(s * rs, (s + 1) * rs)

    @_when(first)
    def _():
        for c in range(nchunk):
            gp_ref[c, :, pad - (kw - 1):pad, :] = fh_ref[:, :, c * d:(c + 1) * d]

    atts = [jnp.dot(load(ya_ref, s), wpa_ref[...], preferred_element_type=F32) for s in subs]
    merged = [(load(m_ref, s).astype(F32) + load(sg_ref, s).astype(F32) * atts[s]).astype(BF16) for s in subs]
    outs = [jnp.dot(merged[s], wo_ref[...], preferred_element_type=F32) for s in subs]
    for s in subs:
        x1 = load(x_ref, s) + outs[s]
        if bb == 1:
            y_ref[0, rsl(s), :] = x1
        else:
            y_ref[...] = x1.reshape(bb, tm, d)
        u2_ref[rsl(s), :] = ((x1 * _rms_scale(x1)) * ln2_ref[...]).astype(BF16)

    def up(c):
        for s in subs:
            gp = jnp.dot(u2_ref[rsl(s), :], wup_ref[:, c * d:(c + 1) * d], preferred_element_type=F32)
            if bb == 1:
                gp_ref[c, 0, pad + s * rs:pad + (s + 1) * rs, :] = gp
            else:
                for b in range(bb):
                    gp_ref[c, b, pad:pad + tm, :] = gp[b * tm:(b + 1) * tm]
            val_ref[c % 2, rsl(s), :] = jnp.dot(u2_ref[rsl(s), :], wup_ref[:, dff + c * d:dff + (c + 1) * d],
                                                preferred_element_type=F32)

    def conv(c, b, start, n):
        cs = slice(c * d, (c + 1) * d)
        off = start - (kw - 1)
        g = gp_ref[c, b, off:off + n, :] * fw_ref[0:1, cs]
        for j in range(1, kw):
            g = g + gp_ref[c, b, off + j:off + j + n, :] * fw_ref[j:j + 1, cs]
        return g + fb_ref[:, cs]

    up(0)
    for c in range(nchunk):
        if c + 1 < nchunk:
            up(c + 1)
        cs = slice(c * d, (c + 1) * d)
        for s in subs:
            if bb == 1:
                gc = conv(c, 0, pad + s * rs, rs)
            else:
                gc = jnp.concatenate([conv(c, b, pad, tm) for b in range(bb)], axis=0)
            act = (_gelu_tanh(gc) * val_ref[c % 2, rsl(s), :]).astype(BF16)
            part = jnp.dot(act, wdn_ref[c * d:(c + 1) * d, :], preferred_element_type=F32)
            if bb == 1:
                y_ref[0, rsl(s), :] += part
            else:
                y_ref[...] += part.reshape(bb, tm, d)
        for b in range(bb):
            tail = gp_ref[c, b, pad + tm - (kw - 1):pad + tm, :]
            ft_ref[b, :, cs] = tail
            gp_ref[c, b, pad - (kw - 1):pad, :] = tail


def _back(prompt, sample, weights, *, tm, nsub):
    nb, t, d = prompt[0].shape
    bs, ts, _ = sample[0].shape
    dff = weights[-1].shape[0]
    kw = weights[4].shape[0]
    nt, nprompt, tile, stream, whole = _stream_specs(nb, t, tm, d)
    ins = tuple(prompt) + tuple(sample) + tuple(weights)

    def body(*refs):
        ins_p, ins_s, w = refs[0:5], refs[5:10], refs[10:10 + len(weights)]
        o = 10 + len(weights)
        outs_p, outs_s, scr_p, scr_s = refs[o:o + 2], refs[o + 2:o + 4], refs[o + 4:o + 7], refs[o + 7:o + 10]
        s = pl.program_id(0)

        @pl.when(s < nprompt)
        def _():
            _back_kernel(*ins_p, *w, *outs_p, *scr_p, bb=1, tm=tm, d=d, dff=dff, nsub=nsub, first=s % nt == 0)

        @pl.when(s == nprompt)
        def _():
            _back_kernel(*ins_s, *w, *outs_s, *scr_s, bb=bs, tm=ts, d=d, dff=dff, nsub=1, first=True)

    out_shape = [jax.ShapeDtypeStruct((nb, t, d), F32), jax.ShapeDtypeStruct((nb, kw - 1, dff), F32),
                 jax.ShapeDtypeStruct((bs, ts, d), F32), jax.ShapeDtypeStruct((bs, kw - 1, dff), F32)]
    return pl.pallas_call(
        body,
        grid=(nprompt + 1,),
        in_specs=[tile] * 4 + [stream(kw - 1, dff)] + [whole(a.shape) for a in ins[5:]],
        out_specs=[tile, stream(kw - 1, dff)] + [whole(o.shape) for o in out_shape[2:]],
        out_shape=out_shape,
        scratch_shapes=[pltpu.VMEM((dff // d, 1, 8 + tm, d), F32), pltpu.VMEM((tm, d), BF16),
                        pltpu.VMEM((2, tm, d), F32),
                        pltpu.VMEM((dff // d, bs, 8 + ts, d), F32), pltpu.VMEM((bs * ts, d), BF16),
                        pltpu.VMEM((2, bs * ts, d), F32)],
        compiler_params=pltpu.CompilerParams(dimension_semantics=("arbitrary",),
                                             vmem_limit_bytes=VMEM_LIMIT_BYTES),
        name="back",
    )(*ins)


def _dense_gate_blocks(w):
    nblk, c, _ = w.shape
    per = GATE_GROUP // c
    w4 = w.reshape(nblk // per, per, c, c)
    eye = jnp.eye(per, dtype=w.dtype)
    return jnp.einsum('gncd,nm->gncmd', w4, eye).reshape(nblk // per, GATE_GROUP, GATE_GROUP)


def kernel(x_prompt, x_sample, cache_k, cache_v, state_rnn_conv, state_rnn_h, state_ffn_conv, ln1, w_in, rnn_conv_w, rnn_conv_b, lru_wa, lru_ba, lru_wx, lru_bx, lru_lambda, q_norm_g, k_norm_g, w_proj_rnn, w_proj_attn, w_out, ln2, w_up, ffn_conv_w, ffn_conv_b, w_down):
    depth = ln1.shape[0]
    assert depth == 1
    bp, tp, d = x_prompt.shape
    bs, ts, _ = x_sample.shape
    dff = w_down.shape[1]
    row = lambda a: a[0].reshape(1, -1)
    wg = jnp.concatenate([_dense_gate_blocks(lru_wa[0]), _dense_gate_blocks(lru_wx[0])], axis=-1).astype(BF16)
    front_w = (row(ln1), w_in[0].astype(BF16), rnn_conv_w[0], row(rnn_conv_b), wg, row(lru_ba), row(lru_bx),
               row(lru_lambda), row(q_norm_g), row(k_norm_g), w_proj_rnn[0].astype(BF16))
    back_w = (w_proj_attn[0].astype(BF16), w_out[0].astype(BF16), row(ln2), w_up[0].astype(BF16),
              ffn_conv_w[0], row(ffn_conv_b), w_down[0].astype(BF16))
    kw = rnn_conv_w.shape[1]
    fkw = ffn_conv_w.shape[1]

    zc = jnp.zeros((bp, kw - 1, d), F32)
    zh = jnp.zeros((bp, 1, d), F32)
    zf = jnp.zeros((bp, fkw - 1, dff), F32)
    (qp, kp, vp, mp, sgp, rcp, hp, qs, ks, vs, ms, sgs, rcs, hs) = _front(
        x_prompt, zc, zh, x_sample, state_rnn_conv[0], state_rnn_h[0].reshape(bs, 1, d), front_w, tm=512, nsub=2)
    yap = _attention(qp, kp, vp, bq=256, bk=256, nq=4, heads=4)
    yas = _attention(qs, ks, vs, cache_k[0].reshape(bs, -1, HEAD_DIM), cache_v[0].reshape(bs, -1, HEAD_DIM),
                     bq=ts, bk=256, nq=1, heads=N_HEADS)
    yp, fcp, ys, fcs = _back((x_prompt, mp, sgp, yap, zf), (x_sample, ms, sgs, yas, state_ffn_conv[0]), back_w,
                             tm=512, nsub=2)
    hd = lambda a: a.reshape(1, a.shape[0], a.shape[1], N_HEADS, HEAD_DIM)
    return (yp, ys, hd(kp), hd(vp), rcp[None], hp.reshape(1, bp, d), fcp[None],
            hd(ks), hd(vs), rcs[None], hs.reshape(1, bs, d), fcs[None])
```

```python
import functools
import math

import jax
import jax.numpy as jnp
from jax import lax
from jax.experimental import pallas as pl
from jax.experimental.pallas import tpu as pltpu

F32 = jnp.float32
BF16 = jnp.bfloat16

NORM_EPS = 1e-6
LRU_C = 8.0
N_HEADS = 8
HEAD_DIM = 128
N_LRU_BLOCKS = 16
LANE = 128
GATE_GROUP = 256
VMEM_LIMIT_BYTES = 56 * 1024 * 1024
SUFFIX_FLOOR = -104.0
STAGE_ROWS, STAGE_COLS = 256, 1024


def _gelu_tanh(x):
    c = math.sqrt(2.0 / math.pi)
    hx = 0.5 * x
    return hx + hx * jnp.tanh(x * (c + (0.044715 * c) * (x * x)))


def _sqrt_nonneg(x):
    return jnp.where(x > 0.0, x * lax.rsqrt(x), 0.0)


def _sigmoid(x):
    return 0.5 * jnp.tanh(0.5 * x) + 0.5


def _load_as_bf16(pairs, stage_ref, sem_ref):
    tiles = []
    for src, dst in pairs:
        k, n = src.shape
        tr, tc = min(k, stage_ref.shape[1]), min(n, stage_ref.shape[2])
        tiles += [(src, dst, r0, c0, tr, tc) for r0 in range(0, k, tr) for c0 in range(0, n, tc)]

    def copy(i):
        src, _, r0, c0, tr, tc = tiles[i]
        return pltpu.make_async_copy(src.at[r0:r0 + tr, c0:c0 + tc], stage_ref.at[i % 2, 0:tr, 0:tc],
                                     sem_ref.at[i % 2])
    copy(0).start()
    for i, (_, dst, r0, c0, tr, tc) in enumerate(tiles):
        if i + 1 < len(tiles):
            copy(i + 1).start()
        copy(i).wait()
        dst[r0:r0 + tr, c0:c0 + tc] = stage_ref[i % 2, 0:tr, 0:tc].astype(BF16)


def _when(cond):
    if isinstance(cond, bool):
        return (lambda f: f()) if cond else (lambda f: None)
    return pl.when(cond)


def _softplus(x):
    return jnp.maximum(x, 0.0) + jnp.log1p(jnp.exp(-jnp.abs(x)))


def _rms_scale(x):
    return lax.rsqrt(jnp.mean(x * x, axis=-1, keepdims=True) + NORM_EPS)


def _head_rmsnorm(x, g):
    outs = []
    for h in range(N_HEADS):
        xh = x[:, h * HEAD_DIM:(h + 1) * HEAD_DIM]
        outs.append(xh * _rms_scale(xh) * g)
    return outs


def _front_kernel(x_ref, hist_ref, h0_ref, ln1_ref, w_in_ref, cw_ref, cb_ref, wg_ref, ba_ref, bx_ref,
                  lam_ref, qg_ref, kg_ref, wpr_ref,
                  q_ref, k_ref, v_ref, m_ref, sg_ref, tail_ref, hl_ref,
                  u_ref, xp_ref, a_ref, b_ref, hc_ref, *, bb, tm, d, first):
    rows = bb * tm
    kw = cw_ref.shape[0]
    pad = 8

    @_when(first)
    def _():
        xp_ref[:, pad - (kw - 1):pad, :] = hist_ref[...]
        hc_ref[...] = h0_ref[...]

    x = x_ref[...].reshape(rows, d)
    u_ref[...] = ((x * _rms_scale(x)) * ln1_ref[...]).astype(BF16)

    def proj(c):
        return jnp.dot(u_ref[...], w_in_ref[:, c * d:(c + 1) * d], preferred_element_type=F32)

    xr = proj(0)
    for b in range(bb):
        xp_ref[b, pad:pad + tm, :] = xr[b * tm:(b + 1) * tm]
    xcs = []
    for b in range(bb):
        off = pad - (kw - 1)
        y = xp_ref[b, off:off + tm, :] * cw_ref[0:1, :]
        for j in range(1, kw):
            y = y + xp_ref[b, off + j:off + j + tm, :] * cw_ref[j:j + 1, :]
        xcs.append(y + cb_ref[...])
        tail = xp_ref[b, pad + tm - (kw - 1):pad + tm, :]
        tail_ref[b] = tail
        xp_ref[b, pad - (kw - 1):pad, :] = tail
    xc = xcs[0] if bb == 1 else jnp.concatenate(xcs, axis=0)

    sp_lam = _softplus(-lam_ref[...])
    for g in range(d // GATE_GROUP):
        sl = slice(g * GATE_GROUP, (g + 1) * GATE_GROUP)
        xg = xc[:, sl]
        pre = jnp.dot(xg.astype(BF16), wg_ref[g], preferred_element_type=F32)
        r = _sigmoid(pre[:, :GATE_GROUP] + ba_ref[:, sl])
        ig = _sigmoid(pre[:, GATE_GROUP:] + bx_ref[:, sl])
        log_a = (-LRU_C) * r * sp_lam[:, sl]
        a = jnp.exp(log_a)
        one_m_a2 = -jnp.tanh(log_a) * (a * a + 1.0)
        a_ref[:, sl] = a
        b_ref[:, sl] = _sqrt_nonneg(one_m_a2) * (ig * xg)

    for b in range(bb):
        def step(t, h, b=b):
            r = b * tm + t
            h = a_ref[pl.ds(r, 1), :] * h + b_ref[pl.ds(r, 1), :]
            b_ref[pl.ds(r, 1), :] = h
            return h
        h_end = lax.fori_loop(0, tm, step, hc_ref[b], unroll=8)
        hc_ref[b] = h_end
        hl_ref[b] = h_end

    y_rnn = (b_ref[...] * _gelu_tanh(proj(1))).astype(BF16)
    m = _sigmoid(proj(5)) * jnp.dot(y_rnn, wpr_ref[...], preferred_element_type=F32)
    m_ref[...] = m.astype(BF16).reshape(bb, tm, d)

    qn = _head_rmsnorm(proj(2), qg_ref[...])
    for h in range(N_HEADS):
        q_ref[:, :, h * HEAD_DIM:(h + 1) * HEAD_DIM] = (qn[h] * HEAD_DIM ** -0.5).astype(BF16).reshape(
            bb, tm, HEAD_DIM)
    kn = _head_rmsnorm(proj(3), kg_ref[...])
    for h in range(N_HEADS):
        k_ref[:, :, h * HEAD_DIM:(h + 1) * HEAD_DIM] = kn[h].reshape(bb, tm, HEAD_DIM)
    v_ref[...] = proj(4).reshape(bb, tm, d)
    sg_ref[...] = _sigmoid(proj(6)).astype(BF16).reshape(bb, tm, d)


def _front_chain_kernel(x_ref, hist_ref, h0_ref, ln1_ref, w_in_ref, cw_ref, cb_ref, wg_ref, ba_ref, bx_ref,
                        lam_ref, qg_ref, kg_ref, wpr_ref,
                        q_ref, k_ref, v_ref, m_ref, sg_ref, tail_ref, hl_ref,
                        u_ref, xp_ref, hc_ref, xc_ref, *fab_refs, tm, d, nsub, first):
    fa_refs, fb_refs = fab_refs[0::2], fab_refs[1::2]
    rs = tm // nsub
    kw = cw_ref.shape[0]
    pad = 8
    nl = d // LANE
    gl = GATE_GROUP // LANE
    subs = range(nsub)

    def rsl(s):
        return slice(s * rs, (s + 1) * rs)

    @_when(first)
    def _():
        xp_ref[0, pad - (kw - 1):pad, :] = hist_ref[0]
        for c in range(nl):
            hc_ref[c:c + 1, :] = h0_ref[0, :, c * LANE:(c + 1) * LANE]

    for s in subs:
        x = x_ref[0, rsl(s), :]
        u_ref[rsl(s), :] = ((x * _rms_scale(x)) * ln1_ref[...]).astype(BF16)

    def proj(c, s):
        return jnp.dot(u_ref[rsl(s), :], w_in_ref[:, c * d:(c + 1) * d], preferred_element_type=F32)

    for s in subs:
        xp_ref[0, pad + s * rs:pad + (s + 1) * rs, :] = proj(0, s)

    ngrp = d // GATE_GROUP
    sp_lam = _softplus(-lam_ref[...])
    gels = [None] * nsub

    def conv(s):
        off = pad + s * rs - (kw - 1)
        xc = xp_ref[0, off:off + rs, :] * cw_ref[0:1, :]
        for j in range(1, kw):
            xc = xc + xp_ref[0, off + j:off + j + rs, :] * cw_ref[j:j + 1, :]
        xc_ref[rsl(s), :] = xc + cb_ref[...]

    def gate(s, g):
        sl = slice(g * GATE_GROUP, (g + 1) * GATE_GROUP)
        xg = xc_ref[rsl(s), sl]
        pre = jnp.dot(xg.astype(BF16), wg_ref[g], preferred_element_type=F32)
        r = _sigmoid(pre[:, :GATE_GROUP] + ba_ref[:, sl])
        ig = _sigmoid(pre[:, GATE_GROUP:] + bx_ref[:, sl])
        log_a = (-LRU_C) * r * sp_lam[:, sl]
        a = jnp.exp(log_a)
        bv = _sqrt_nonneg(-jnp.tanh(log_a) * (a * a + 1.0)) * (ig * xg)
        for rg in range(rs // 8):
            for cl in range(gl):
                dst = pl.ds(8 * rg * nl + g * gl + cl, 8, stride=nl)
                fa_refs[s][dst, :] = a[8 * rg:8 * rg + 8, cl * LANE:(cl + 1) * LANE]
                fb_refs[s][dst, :] = bv[8 * rg:8 * rg + 8, cl * LANE:(cl + 1) * LANE]

    def scan(s):
        h = hc_ref[...]
        for t in range(rs):
            h = fa_refs[s][t * nl:(t + 1) * nl, :] * h + fb_refs[s][t * nl:(t + 1) * nl, :]
            fb_refs[s][t * nl:(t + 1) * nl, :] = h
        hc_ref[...] = h

    def keys(s):
        kn = _head_rmsnorm(proj(3, s), kg_ref[...])
        for h in range(N_HEADS):
            k_ref[0, rsl(s), h * HEAD_DIM:(h + 1) * HEAD_DIM] = kn[h]

    def values(s):
        v_ref[0, rsl(s), :] = proj(4, s)

    def queries(s):
        qn = _head_rmsnorm(proj(2, s), qg_ref[...])
        for h in range(N_HEADS):
            q_ref[0, rsl(s), h * HEAD_DIM:(h + 1) * HEAD_DIM] = (qn[h] * HEAD_DIM ** -0.5).astype(BF16)

    def gelu_gate(s):
        gels[s] = _gelu_tanh(proj(1, s))

    wide = [f for s in subs for f in (keys, values, queries)] + [gelu_gate] * nsub
    wide_args = [s for s in subs for _ in range(3)] + list(subs)
    small = [(s, g) for s in subs for g in range(ngrp)]
    conv(0)
    for n in range(max(len(wide), len(small))):
        if n < len(wide):
            wide[n](wide_args[n])
        if n < len(small):
            s, g = small[n]
            if g == ngrp - 1 and s + 1 < nsub:
                conv(s + 1)
            gate(s, g)
            if g == ngrp - 1:
                scan(s)
    tail = xp_ref[0, pad + tm - (kw - 1):pad + tm, :]
    tail_ref[0] = tail
    xp_ref[0, pad - (kw - 1):pad, :] = tail
    h_end = hc_ref[...]
    for c in range(nl):
        hl_ref[0, :, c * LANE:(c + 1) * LANE] = h_end[c:c + 1, :]

    g_rnns = [_sigmoid(proj(5, s)) for s in subs]
    for s in subs:
        sg_ref[0, rsl(s), :] = _sigmoid(proj(6, s)).astype(BF16)
        hseq = jnp.concatenate(
            [jnp.concatenate([fb_refs[s][pl.ds(8 * rg * nl + c, 8, stride=nl), :] for c in range(nl)], axis=1)
             for rg in range(rs // 8)], axis=0)
        y_rnn = (hseq * gels[s]).astype(BF16)
        m = g_rnns[s] * jnp.dot(y_rnn, wpr_ref[...], preferred_element_type=F32)
        m_ref[0, rsl(s), :] = m.astype(BF16)


def _staging():
    return [pltpu.VMEM((2, STAGE_ROWS, STAGE_COLS), F32), pltpu.SemaphoreType.DMA((2,))]


def _stream_specs(nb, t, tm, d):
    nt = t // tm
    nprompt = nb * nt

    def where(s):
        c = jnp.minimum(s, nprompt - 1)
        return c // nt, c % nt
    tile = pl.BlockSpec((1, tm, d), lambda s: (*where(s), 0))
    stream = lambda n, w: pl.BlockSpec((1, n, w), lambda s: (where(s)[0], 0, 0))
    whole = lambda shape: pl.BlockSpec(tuple(shape), lambda s: (0,) * len(shape))
    return nt, nprompt, tile, stream, whole


def _front(xp, hist_p, h0_p, xs, hist_s, h0_s, weights, *, tm, nsub):
    nb, t, d = xp.shape
    bs, ts, _ = xs.shape
    kw = hist_p.shape[1] + 1
    nt, nprompt, tile, stream, whole = _stream_specs(nb, t, tm, d)
    ins = (xp, hist_p, h0_p, xs, hist_s, h0_s) + tuple(weights)
    big = (1, 4, 10)

    def body(*refs):
        ins_p, ins_s, w = refs[0:3], refs[3:6], list(refs[6:6 + len(weights)])
        o = 6 + len(weights)
        nscr = 4 + 2 * nsub
        outs_p, outs_s = refs[o:o + 7], refs[o + 7:o + 14]
        scr_p, scr_s = refs[o + 14:o + 14 + nscr], refs[o + 14 + nscr:o + 19 + nscr]
        wbuf, (stage_ref, sem_ref) = refs[o + 19 + nscr:o + 22 + nscr], refs[o + 22 + nscr:]
        s = pl.program_id(0)

        @pl.when(s == 0)
        def _():
            gates = [(w[4].at[g], wbuf[1].at[g]) for g in range(w[4].shape[0])]
            _load_as_bf16([(w[1], wbuf[0])] + gates + [(w[10], wbuf[2])], stage_ref, sem_ref)
        for n, buf in zip(big, wbuf):
            w[n] = buf

        @pl.when(s < nprompt)
        def _():
            _front_chain_kernel(*ins_p, *w, *outs_p, *scr_p, tm=tm, d=d, nsub=nsub, first=s % nt == 0)

        @pl.when(s == nprompt)
        def _():
            _front_kernel(*ins_s, *w, *outs_s, *scr_s, bb=bs, tm=ts, d=d, first=True)

    def outs(n, rows):
        act = lambda dt: jax.ShapeDtypeStruct((n, rows, d), dt)
        return [act(BF16), act(F32), act(F32), act(BF16), act(BF16),
                jax.ShapeDtypeStruct((n, kw - 1, d), F32), jax.ShapeDtypeStruct((n, 1, d), F32)]
    out_shape = outs(nb, t) + outs(bs, ts)
    return pl.pallas_call(
        body,
        grid=(nprompt + 1,),
        in_specs=[tile, stream(kw - 1, d), stream(1, d)] + [whole(a.shape) for a in ins[3:6]]
        + [pl.BlockSpec(memory_space=pl.ANY) if n in big else whole(a.shape) for n, a in enumerate(weights)],
        out_specs=[tile] * 5 + [stream(kw - 1, d), stream(1, d)] + [whole(o.shape) for o in out_shape[7:]],
        out_shape=out_shape,
        scratch_shapes=[pltpu.VMEM((tm, d), BF16), pltpu.VMEM((1, 8 + tm, d), F32),
                        pltpu.VMEM((d // LANE, LANE), F32), pltpu.VMEM((tm, d), F32)]
        + [pltpu.VMEM((tm // nsub * d // LANE, LANE), F32)] * (2 * nsub)
        + [pltpu.VMEM((bs * ts, d), BF16), pltpu.VMEM((bs, 8 + ts, d), F32),
                        pltpu.VMEM((bs * ts, d), F32), pltpu.VMEM((bs * ts, d), F32), pltpu.VMEM((bs, 1, d), F32)]
        + [pltpu.VMEM(weights[n].shape, BF16) for n in big] + _staging(),
        compiler_params=pltpu.CompilerParams(dimension_semantics=("arbitrary",),
                                             vmem_limit_bytes=VMEM_LIMIT_BYTES),
        name="front",
    )(*ins)


def _attn_kernel(*refs, bq, bk, nq, heads, has_cache):
    if has_cache:
        q_ref, kn_ref, vn_ref, kp_ref, vp_ref, u_ref, o_ref, acc_ref, r_ref, alive_ref, kpad_ref, vpad_ref = refs
    else:
        q_ref, kn_ref, vn_ref, u_ref, o_ref, acc_ref, r_ref, alive_ref = refs
        kp_ref, vp_ref = kn_ref, vn_ref
    rh = min(bq, LANE)
    ngrp = bq // rh
    chains = [(h, g) for h in range(heads) for g in range(ngrp)]

    def head_rows(ref, start, n, h):
        if ref.shape[-1] == HEAD_DIM and heads > 1:
            return ref[0, pl.ds(start * N_HEADS + h, n, stride=N_HEADS), :]
        lanes = slice(h * HEAD_DIM, (h + 1) * HEAD_DIM)
        return ref[pl.ds(start, n), lanes] if len(ref.shape) == 2 else ref[0, pl.ds(start, n), lanes]

    def q_block(qi):
        i = pl.program_id(2) * nq + qi
        qoff = 0 if nq == 1 else pl.multiple_of(qi * bq, bq)

        def run(tiles, state):
            zs = []
            for c, k_ref, _, start, nk, _ in tiles:
                h, g = chains[c]
                qh = q_ref[0, pl.ds(qoff + g * rh, rh), h * HEAD_DIM:(h + 1) * HEAD_DIM]
                zs.append(lax.dot_general(qh, head_rows(k_ref, start, nk, h).astype(BF16), (((1,), (1,)), ((), ())),
                                          preferred_element_type=F32))
            log_betas, log_1ms = [], []
            for (_, _, _, _, _, valid), z in zip(tiles, zs):
                log_beta = jnp.minimum(z, 0.0) - jnp.log(1.0 + jnp.exp(-jnp.abs(z)))
                log_1m = log_beta - z
                log_betas.append(log_beta)
                log_1ms.append(log_1m if valid is None else jnp.where(valid, log_1m, 0.0))
            suffixes = [jnp.dot(l.astype(BF16), u_ref[0:t[4], 0:t[4]], preferred_element_type=F32)
                        for t, l in zip(tiles, log_1ms)]
            state = list(state)
            ws = []
            for n, (c, _, _, _, nk, valid) in enumerate(tiles):
                arg = log_betas[n] + suffixes[n]
                rs = jnp.broadcast_to(jnp.sum(log_1ms[n], axis=-1, keepdims=True), (rh, LANE))
                if state[c] is None:
                    state[c] = (rs, None)
                else:
                    r, acc = state[c]
                    arg = arg + (r if nk == LANE else jnp.concatenate([r] * (nk // LANE), axis=1))
                    state[c] = (r + rs, acc)
                w = jnp.exp(arg)
                ws.append((w if valid is None else jnp.where(valid, w, 0.0)).astype(BF16))
            for n, (c, _, v_ref, start, nk, _) in enumerate(tiles):
                pv = jnp.dot(ws[n], head_rows(v_ref, start, nk, chains[c][0]).astype(BF16),
                             preferred_element_type=F32)
                r, acc = state[c]
                state[c] = (r, pv if acc is None else acc + pv)
            return state

        def save(state):
            for c, entry in enumerate(state):
                if entry is not None:
                    r_ref[c], acc_ref[c] = entry

        def load(only=None):
            return [(r_ref[c], acc_ref[c]) if only is None or only(g) else None for c, (_, g) in enumerate(chains)]

        def live():
            return jnp.max(r_ref[...]) > SUFFIX_FLOOR

        def first(tiles):
            state = run(tiles, [None] * len(chains))
            save(state)
            top = functools.reduce(jnp.maximum, [r for r, _ in state])
            alive_ref[0] = (jnp.max(top) > SUFFIX_FLOOR).astype(jnp.int32)

        if has_cache:
            tn = kn_ref.shape[1]
            for pad_ref, src_ref in ((kpad_ref, kn_ref), (vpad_ref, vn_ref)):
                pad_ref[...] = jnp.zeros_like(pad_ref)
                pad_ref[0:tn, :] = src_ref[0]
            kd_ref, vd_ref, dstart = kpad_ref, vpad_ref, 0
        else:
            kd_ref, vd_ref, dstart = kn_ref, vn_ref, pl.multiple_of(i * bq, bq)
        diag = []
        for c, (h, g) in enumerate(chains):
            nk = -(-((g + 1) * rh) // LANE) * LANE
            row = lax.broadcasted_iota(jnp.int32, (rh, nk), 0) + g * rh
            col = lax.broadcasted_iota(jnp.int32, (rh, nk), 1)
            diag.append((c, kd_ref, vd_ref, dstart, nk, col < row))

        def past(j):
            start = j * bk if isinstance(j, int) else pl.multiple_of(j * bk, bk)
            return [(c, kp_ref, vp_ref, start, bk, None) for c in range(len(chains))]

        if has_cache:
            j0 = kp_ref.shape[1] // N_HEADS // bk - 1
            first(diag + past(j0))
            j1 = jnp.int32(j0 - 1)
        else:
            def earlier(lo, n):
                return pl.multiple_of(dstart - bk + lo, LANE), n

            @pl.when(i > 0)
            def _():
                first(diag + [(c, kp_ref, vp_ref) + earlier(g * rh, bk - g * rh) + (None,)
                              for c, (_, g) in enumerate(chains)])

            @pl.when(i == 0)
            def _():
                first(diag)
                alive_ref[0] = 0
            j1 = i - 2

        @pl.when(alive_ref[0] > 0)
        def _():
            if not has_cache and ngrp > 1:
                save(run([(c, kp_ref, vp_ref) + earlier(0, g * rh) + (None,)
                          for c, (_, g) in enumerate(chains) if g > 0], load(lambda g: g > 0)))

            def cond(c):
                return jnp.logical_and(c[0] >= 0, c[1])

            def body(c):
                save(run(past(c[0]), load()))
                return c[0] - 1, live()
            lax.while_loop(cond, body, (j1, live()))

        for c, (h, g) in enumerate(chains):
            o_ref[0, pl.ds(qoff + g * rh, rh), h * HEAD_DIM:(h + 1) * HEAD_DIM] = acc_ref[c].astype(o_ref.dtype)

    if nq == 1:
        q_block(0)
    else:
        def q_step(qi, carry):
            q_block(qi)
            return carry
        lax.fori_loop(0, nq, q_step, 0)


def _attention(q, kn, vn, kc=None, vc=None, *, bq, bk, nq, heads):
    nb, t, d = q.shape
    hw = heads * HEAD_DIM
    has_cache = kc is not None
    assert has_cache or (bk == bq and bq % LANE == 0)
    assert not has_cache or (heads == N_HEADS and kc.shape[2] == HEAD_DIM and t == bq <= LANE and nq == 1)
    rh = min(bq, LANE)
    nchain = heads * (bq // rh)
    grid = (nb, d // hw, t // (bq * nq))
    qspec = pl.BlockSpec((1, bq * nq, hw), lambda b, g, i: (b, i, g))
    kvspec = lambda a: pl.BlockSpec((1, a.shape[1], hw), lambda b, g, i: (b, 0, g))
    cachespec = lambda a: pl.BlockSpec((1,) + a.shape[1:], lambda b, g, i: (b, 0, 0))
    u = jnp.tril(jnp.ones((bk, bk), F32), -1).astype(BF16)
    kv = [kn, vn] + ([kc, vc] if has_cache else [])
    scratch = [pltpu.VMEM((nchain, rh, HEAD_DIM), F32), pltpu.VMEM((nchain, rh, LANE), F32),
               pltpu.SMEM((1,), jnp.int32)]
    if has_cache:
        scratch += [pltpu.VMEM((LANE, hw), F32), pltpu.VMEM((LANE, hw), F32)]
    return pl.pallas_call(
        functools.partial(_attn_kernel, bq=bq, bk=bk, nq=nq, heads=heads, has_cache=has_cache),
        grid=grid,
        in_specs=[qspec, kvspec(kn), kvspec(vn)] + [cachespec(a) for a in kv[2:]]
        + [pl.BlockSpec(u.shape, lambda b, g, i: (0, 0))],
        out_specs=qspec,
        out_shape=jax.ShapeDtypeStruct((nb, t, d), BF16),
        scratch_shapes=scratch,
        compiler_params=pltpu.CompilerParams(dimension_semantics=("arbitrary", "arbitrary", "arbitrary"),
                                             vmem_limit_bytes=VMEM_LIMIT_BYTES),
        name="attn_cache" if has_cache else "attn",
    )(q, *kv, u)


def _back_kernel(x_ref, m_ref, sg_ref, ya_ref, fh_ref, wpa_ref, wo_ref, ln2_ref, wup_ref, fw_ref, fb_ref, wdn_ref,
                 y_ref, ft_ref, gp_ref, u2_ref, val_ref, *, bb, tm, d, dff, nsub, first):
    rows = bb * tm
    rs = rows // nsub
    kw = fw_ref.shape[0]
    pad = 8
    nchunk = dff // d
    subs = range(nsub)

    def load(ref, s):
        return ref[0, s * rs:(s + 1) * rs, :] if bb == 1 else ref[...].reshape(rows, d)

    def rsl(s):
        return slice(s * rs, (s + 1) * rs)

    @_when(first)
    def _():
        for c in range(nchunk):
            gp_ref[c, :, pad - (kw - 1):pad, :] = fh_ref[:, :, c * d:(c + 1) * d]

    atts = [jnp.dot(load(ya_ref, s), wpa_ref[...], preferred_element_type=F32) for s in subs]
    merged = [(load(m_ref, s).astype(F32) + load(sg_ref, s).astype(F32) * atts[s]).astype(BF16) for s in subs]
    outs = [jnp.dot(merged[s], wo_ref[...], preferred_element_type=F32) for s in subs]
    for s in subs:
        x1 = load(x_ref, s) + outs[s]
        if bb == 1:
            y_ref[0, rsl(s), :] = x1
        else:
            y_ref[...] = x1.reshape(bb, tm, d)
        u2_ref[rsl(s), :] = ((x1 * _rms_scale(x1)) * ln2_ref[...]).astype(BF16)

    def up(c):
        for s in subs:
            gp = jnp.dot(u2_ref[rsl(s), :], wup_ref[:, c * d:(c + 1) * d], preferred_element_type=F32)
            if bb == 1:
                gp_ref[c, 0, pad + s * rs:pad + (s + 1) * rs, :] = gp
            else:
                for b in range(bb):
                    gp_ref[c, b, pad:pad + tm, :] = gp[b * tm:(b + 1) * tm]
            val_ref[c % 2, rsl(s), :] = jnp.dot(u2_ref[rsl(s), :], wup_ref[:, dff + c * d:dff + (c + 1) * d],
                                                preferred_element_type=F32)

    def conv(c, b, start, n):
        cs = slice(c * d, (c + 1) * d)
        off = start - (kw - 1)
        g = gp_ref[c, b, off:off + n, :] * fw_ref[0:1, cs]
        for j in range(1, kw):
            g = g + gp_ref[c, b, off + j:off + j + n, :] * fw_ref[j:j + 1, cs]
        return g + fb_ref[:, cs]

    up(0)
    for c in range(nchunk):
        if c + 1 < nchunk:
            up(c + 1)
        cs = slice(c * d, (c + 1) * d)
        for s in subs:
            if bb == 1:
                gc = conv(c, 0, pad + s * rs, rs)
            else:
                gc = jnp.concatenate([conv(c, b, pad, tm) for b in range(bb)], axis=0)
            act = (_gelu_tanh(gc) * val_ref[c % 2, rsl(s), :]).astype(BF16)
            part = jnp.dot(act, wdn_ref[c * d:(c + 1) * d, :], preferred_element_type=F32)
            if bb == 1:
                y_ref[0, rsl(s), :] += part
            else:
                y_ref[...] += part.reshape(bb, tm, d)
        for b in range(bb):
            tail = gp_ref[c, b, pad + tm - (kw - 1):pad + tm, :]
            ft_ref[b, :, cs] = tail
            gp_ref[c, b, pad - (kw - 1):pad, :] = tail


def _back(prompt, sample, weights, *, tm, nsub):
    nb, t, d = prompt[0].shape
    bs, ts, _ = sample[0].shape
    dff = weights[-1].shape[0]
    kw = weights[4].shape[0]
    nt, nprompt, tile, stream, whole = _stream_specs(nb, t, tm, d)
    ins = tuple(prompt) + tuple(sample) + tuple(weights)
    big = (0, 1, 3, 6)

    def body(*refs):
        ins_p, ins_s, w = refs[0:5], refs[5:10], list(refs[10:10 + len(weights)])
        o = 10 + len(weights)
        outs_p, outs_s, scr_p, scr_s = refs[o:o + 2], refs[o + 2:o + 4], refs[o + 4:o + 7], refs[o + 7:o + 10]
        wbuf, (stage_ref, sem_ref) = refs[o + 10:o + 14], refs[o + 14:]
        s = pl.program_id(0)

        @pl.when(s == 0)
        def _():
            _load_as_bf16([(w[n], buf) for n, buf in zip(big, wbuf)], stage_ref, sem_ref)
        for n, buf in zip(big, wbuf):
            w[n] = buf

        @pl.when(s < nprompt)
        def _():
            _back_kernel(*ins_p, *w, *outs_p, *scr_p, bb=1, tm=tm, d=d, dff=dff, nsub=nsub, first=s % nt == 0)

        @pl.when(s == nprompt)
        def _():
            _back_kernel(*ins_s, *w, *outs_s, *scr_s, bb=bs, tm=ts, d=d, dff=dff, nsub=1, first=True)

    out_shape = [jax.ShapeDtypeStruct((nb, t, d), F32), jax.ShapeDtypeStruct((nb, kw - 1, dff), F32),
                 jax.ShapeDtypeStruct((bs, ts, d), F32), jax.ShapeDtypeStruct((bs, kw - 1, dff), F32)]
    return pl.pallas_call(
        body,
        grid=(nprompt + 1,),
        in_specs=[tile] * 4 + [stream(kw - 1, dff)] + [whole(a.shape) for a in ins[5:10]]
        + [pl.BlockSpec(memory_space=pl.ANY) if n in big else whole(a.shape) for n, a in enumerate(weights)],
        out_specs=[tile, stream(kw - 1, dff)] + [whole(o.shape) for o in out_shape[2:]],
        out_shape=out_shape,
        scratch_shapes=[pltpu.VMEM((dff // d, 1, 8 + tm, d), F32), pltpu.VMEM((tm, d), BF16),
                        pltpu.VMEM((2, tm, d), F32),
                        pltpu.VMEM((dff // d, bs, 8 + ts, d), F32), pltpu.VMEM((bs * ts, d), BF16),
                        pltpu.VMEM((2, bs * ts, d), F32)]
        + [pltpu.VMEM(weights[n].shape, BF16) for n in big] + _staging(),
        compiler_params=pltpu.CompilerParams(dimension_semantics=("arbitrary",),
                                             vmem_limit_bytes=VMEM_LIMIT_BYTES),
        name="back",
    )(*ins)


def _dense_gate_blocks(w):
    nblk, c, _ = w.shape
    per = GATE_GROUP // c
    w4 = w.reshape(nblk // per, per, c, c)
    eye = jnp.eye(per, dtype=w.dtype)
    return jnp.einsum('gncd,nm->gncmd', w4, eye).reshape(nblk // per, GATE_GROUP, GATE_GROUP)


def kernel(x_prompt, x_sample, cache_k, cache_v, state_rnn_conv, state_rnn_h, state_ffn_conv, ln1, w_in, rnn_conv_w, rnn_conv_b, lru_wa, lru_ba, lru_wx, lru_bx, lru_lambda, q_norm_g, k_norm_g, w_proj_rnn, w_proj_attn, w_out, ln2, w_up, ffn_conv_w, ffn_conv_b, w_down):
    depth = ln1.shape[0]
    assert depth == 1
    bp, tp, d = x_prompt.shape
    bs, ts, _ = x_sample.shape
    dff = w_down.shape[1]
    row = lambda a: a[0].reshape(1, -1)
    wg = jnp.concatenate([_dense_gate_blocks(lru_wa[0]), _dense_gate_blocks(lru_wx[0])], axis=-1)
    front_w = (row(ln1), w_in[0], rnn_conv_w[0], row(rnn_conv_b), wg, row(lru_ba), row(lru_bx),
               row(lru_lambda), row(q_norm_g), row(k_norm_g), w_proj_rnn[0])
    back_w = (w_proj_attn[0], w_out[0], row(ln2), w_up[0],
              ffn_conv_w[0], row(ffn_conv_b), w_down[0])
    kw = rnn_conv_w.shape[1]
    fkw = ffn_conv_w.shape[1]

    zc = jnp.zeros((bp, kw - 1, d), F32)
    zh = jnp.zeros((bp, 1, d), F32)
    zf = jnp.zeros((bp, fkw - 1, dff), F32)
    (qp, kp, vp, mp, sgp, rcp, hp, qs, ks, vs, ms, sgs, rcs, hs) = _front(
        x_prompt, zc, zh, x_sample, state_rnn_conv[0], state_rnn_h[0].reshape(bs, 1, d), front_w, tm=512, nsub=2)
    yap = _attention(qp, kp, vp, bq=256, bk=256, nq=4, heads=4)
    yas = _attention(qs, ks, vs, cache_k[0].reshape(bs, -1, HEAD_DIM), cache_v[0].reshape(bs, -1, HEAD_DIM),
                     bq=ts, bk=256, nq=1, heads=N_HEADS)
    yp, fcp, ys, fcs = _back((x_prompt, mp, sgp, yap, zf), (x_sample, ms, sgs, yas, state_ffn_conv[0]), back_w,
                             tm=512, nsub=2)
    hd = lambda a: a.reshape(1, a.shape[0], a.shape[1], N_HEADS, HEAD_DIM)
    return (yp, ys, hd(kp), hd(vp), rcp[None], hp.reshape(1, bp, d), fcp[None],
            hd(ks), hd(vs), rcs[None], hs.reshape(1, bs, d), fcs[None])
```

```python
import functools
import math

import jax
import jax.numpy as jnp
from jax import lax
from jax.experimental import pallas as pl
from jax.experimental.pallas import tpu as pltpu

F32 = jnp.float32
BF16 = jnp.bfloat16

NORM_EPS = 1e-6
LRU_C = 8.0
N_HEADS = 8
HEAD_DIM = 128
N_LRU_BLOCKS = 16
LANE = 128
GATE_GROUP = 256
VMEM_LIMIT_BYTES = 56 * 1024 * 1024
SUFFIX_FLOOR = -104.0
STAGE_ROWS, STAGE_COLS = 256, 1024


def _gelu_tanh(x):
    c = math.sqrt(2.0 / math.pi)
    hx = 0.5 * x
    return hx + hx * jnp.tanh(x * (c + (0.044715 * c) * (x * x)))


def _sqrt_nonneg(x):
    return jnp.where(x > 0.0, x * lax.rsqrt(x), 0.0)


def _sigmoid(x):
    return 0.5 * jnp.tanh(0.5 * x) + 0.5


def _load_as_bf16(pairs, slots, sem_ref):
    nslot = len(slots)
    tiles = []
    for src, dst in pairs:
        k, n = src.shape
        tr, tc = min(k, STAGE_ROWS), min(n, STAGE_COLS)
        tiles += [(src, dst, r0, c0, tr, tc) for r0 in range(0, k, tr) for c0 in range(0, n, tc)]

    def copy(i):
        src, _, r0, c0, tr, tc = tiles[i]
        return pltpu.make_async_copy(src.at[r0:r0 + tr, c0:c0 + tc], slots[i % nslot].at[0:tr, 0:tc],
                                     sem_ref.at[i % nslot])
    for i in range(min(nslot - 1, len(tiles))):
        copy(i).start()
    for i, (_, dst, r0, c0, tr, tc) in enumerate(tiles):
        if i + nslot - 1 < len(tiles):
            copy(i + nslot - 1).start()
        copy(i).wait()
        dst[r0:r0 + tr, c0:c0 + tc] = slots[i % nslot][0:tr, 0:tc].astype(BF16)


def _when(cond):
    if isinstance(cond, bool):
        return (lambda f: f()) if cond else (lambda f: None)
    return pl.when(cond)


def _softplus(x):
    return jnp.maximum(x, 0.0) + jnp.log1p(jnp.exp(-jnp.abs(x)))


def _rms_scale(x):
    return lax.rsqrt(jnp.mean(x * x, axis=-1, keepdims=True) + NORM_EPS)


def _head_rmsnorm(x, g):
    outs = []
    for h in range(N_HEADS):
        xh = x[:, h * HEAD_DIM:(h + 1) * HEAD_DIM]
        outs.append(xh * _rms_scale(xh) * g)
    return outs


def _front_kernel(x_ref, hist_ref, h0_ref, ln1_ref, w_in_ref, cw_ref, cb_ref, wg_ref, ba_ref, bx_ref,
                  lam_ref, qg_ref, kg_ref, wpr_ref,
                  q_ref, k_ref, v_ref, m_ref, sg_ref, tail_ref, hl_ref,
                  u_ref, xp_ref, a_ref, b_ref, hc_ref, *, bb, tm, d, first):
    rows = bb * tm
    kw = cw_ref.shape[0]
    pad = 8

    @_when(first)
    def _():
        xp_ref[:, pad - (kw - 1):pad, :] = hist_ref[...]
        hc_ref[...] = h0_ref[...]

    x = x_ref[...].reshape(rows, d)
    u_ref[...] = ((x * _rms_scale(x)) * ln1_ref[...]).astype(BF16)

    def proj(c):
        return jnp.dot(u_ref[...], w_in_ref[:, c * d:(c + 1) * d], preferred_element_type=F32)

    xr = proj(0)
    for b in range(bb):
        xp_ref[b, pad:pad + tm, :] = xr[b * tm:(b + 1) * tm]
    xcs = []
    for b in range(bb):
        off = pad - (kw - 1)
        y = xp_ref[b, off:off + tm, :] * cw_ref[0:1, :]
        for j in range(1, kw):
            y = y + xp_ref[b, off + j:off + j + tm, :] * cw_ref[j:j + 1, :]
        xcs.append(y + cb_ref[...])
        tail = xp_ref[b, pad + tm - (kw - 1):pad + tm, :]
        tail_ref[b] = tail
        xp_ref[b, pad - (kw - 1):pad, :] = tail
    xc = xcs[0] if bb == 1 else jnp.concatenate(xcs, axis=0)

    sp_lam = _softplus(-lam_ref[...])
    for g in range(d // GATE_GROUP):
        sl = slice(g * GATE_GROUP, (g + 1) * GATE_GROUP)
        xg = xc[:, sl]
        pre = jnp.dot(xg.astype(BF16), wg_ref[g], preferred_element_type=F32)
        r = _sigmoid(pre[:, :GATE_GROUP] + ba_ref[:, sl])
        ig = _sigmoid(pre[:, GATE_GROUP:] + bx_ref[:, sl])
        log_a = (-LRU_C) * r * sp_lam[:, sl]
        a = jnp.exp(log_a)
        one_m_a2 = -jnp.tanh(log_a) * (a * a + 1.0)
        a_ref[:, sl] = a
        b_ref[:, sl] = _sqrt_nonneg(one_m_a2) * (ig * xg)

    for b in range(bb):
        def step(t, h, b=b):
            r = b * tm + t
            h = a_ref[pl.ds(r, 1), :] * h + b_ref[pl.ds(r, 1), :]
            b_ref[pl.ds(r, 1), :] = h
            return h
        h_end = lax.fori_loop(0, tm, step, hc_ref[b], unroll=8)
        hc_ref[b] = h_end
        hl_ref[b] = h_end

    y_rnn = (b_ref[...] * _gelu_tanh(proj(1))).astype(BF16)
    m = _sigmoid(proj(5)) * jnp.dot(y_rnn, wpr_ref[...], preferred_element_type=F32)
    m_ref[...] = m.astype(BF16).reshape(bb, tm, d)

    qn = _head_rmsnorm(proj(2), qg_ref[...])
    for h in range(N_HEADS):
        q_ref[:, :, h * HEAD_DIM:(h + 1) * HEAD_DIM] = (qn[h] * HEAD_DIM ** -0.5).astype(BF16).reshape(
            bb, tm, HEAD_DIM)
    kn = _head_rmsnorm(proj(3), kg_ref[...])
    for h in range(N_HEADS):
        k_ref[:, :, h * HEAD_DIM:(h + 1) * HEAD_DIM] = kn[h].reshape(bb, tm, HEAD_DIM)
    v_ref[...] = proj(4).reshape(bb, tm, d)
    sg_ref[...] = _sigmoid(proj(6)).astype(BF16).reshape(bb, tm, d)


def _front_chain_kernel(x_ref, hist_ref, h0_ref, ln1_ref, w_in_ref, cw_ref, cb_ref, wg_ref, ba_ref, bx_ref,
                        lam_ref, qg_ref, kg_ref, wpr_ref,
                        q_ref, k_ref, v_ref, m_ref, sg_ref, tail_ref, hl_ref,
                        u_ref, xp_ref, hc_ref, xc_ref, *fab_refs, tm, d, nsub, first):
    fa_refs, fb_refs = fab_refs[0::2], fab_refs[1::2]
    rs = tm // nsub
    kw = cw_ref.shape[0]
    pad = 8
    nl = d // LANE
    gl = GATE_GROUP // LANE
    subs = range(nsub)

    def rsl(s):
        return slice(s * rs, (s + 1) * rs)

    @_when(first)
    def _():
        xp_ref[0, pad - (kw - 1):pad, :] = hist_ref[0]
        for c in range(nl):
            hc_ref[c:c + 1, :] = h0_ref[0, :, c * LANE:(c + 1) * LANE]

    for s in subs:
        x = x_ref[0, rsl(s), :]
        u_ref[rsl(s), :] = ((x * _rms_scale(x)) * ln1_ref[...]).astype(BF16)

    def proj(c, s):
        return jnp.dot(u_ref[rsl(s), :], w_in_ref[:, c * d:(c + 1) * d], preferred_element_type=F32)

    for s in subs:
        xp_ref[0, pad + s * rs:pad + (s + 1) * rs, :] = proj(0, s)

    ngrp = d // GATE_GROUP
    sp_lam = _softplus(-lam_ref[...])
    gels = [None] * nsub

    def conv(s):
        off = pad + s * rs - (kw - 1)
        xc = xp_ref[0, off:off + rs, :] * cw_ref[0:1, :]
        for j in range(1, kw):
            xc = xc + xp_ref[0, off + j:off + j + rs, :] * cw_ref[j:j + 1, :]
        xc_ref[rsl(s), :] = xc + cb_ref[...]

    def gate(s, g):
        sl = slice(g * GATE_GROUP, (g + 1) * GATE_GROUP)
        xg = xc_ref[rsl(s), sl]
        pre = jnp.dot(xg.astype(BF16), wg_ref[g], preferred_element_type=F32)
        r = _sigmoid(pre[:, :GATE_GROUP] + ba_ref[:, sl])
        ig = _sigmoid(pre[:, GATE_GROUP:] + bx_ref[:, sl])
        log_a = (-LRU_C) * r * sp_lam[:, sl]
        a = jnp.exp(log_a)
        bv = _sqrt_nonneg(-jnp.tanh(log_a) * (a * a + 1.0)) * (ig * xg)
        for rg in range(rs // 8):
            for cl in range(gl):
                dst = pl.ds(8 * rg * nl + g * gl + cl, 8, stride=nl)
                fa_refs[s][dst, :] = a[8 * rg:8 * rg + 8, cl * LANE:(cl + 1) * LANE]
                fb_refs[s][dst, :] = bv[8 * rg:8 * rg + 8, cl * LANE:(cl + 1) * LANE]

    def scan(s):
        h = hc_ref[...]
        for t in range(rs):
            h = fa_refs[s][t * nl:(t + 1) * nl, :] * h + fb_refs[s][t * nl:(t + 1) * nl, :]
            fb_refs[s][t * nl:(t + 1) * nl, :] = h
        hc_ref[...] = h

    def keys(s):
        kn = _head_rmsnorm(proj(3, s), kg_ref[...])
        for h in range(N_HEADS):
            k_ref[0, rsl(s), h * HEAD_DIM:(h + 1) * HEAD_DIM] = kn[h]

    def values(s):
        v_ref[0, rsl(s), :] = proj(4, s)

    def queries(s):
        qn = _head_rmsnorm(proj(2, s), qg_ref[...])
        for h in range(N_HEADS):
            q_ref[0, rsl(s), h * HEAD_DIM:(h + 1) * HEAD_DIM] = (qn[h] * HEAD_DIM ** -0.5).astype(BF16)

    def gelu_gate(s):
        gels[s] = _gelu_tanh(proj(1, s))

    wide = [f for s in subs for f in (keys, values, queries)] + [gelu_gate] * nsub
    wide_args = [s for s in subs for _ in range(3)] + list(subs)
    small = [(s, g) for s in subs for g in range(ngrp)]
    conv(0)
    for n in range(max(len(wide), len(small))):
        if n < len(wide):
            wide[n](wide_args[n])
        if n < len(small):
            s, g = small[n]
            if g == ngrp - 1 and s + 1 < nsub:
                conv(s + 1)
            gate(s, g)
            if g == ngrp - 1:
                scan(s)
    tail = xp_ref[0, pad + tm - (kw - 1):pad + tm, :]
    tail_ref[0] = tail
    xp_ref[0, pad - (kw - 1):pad, :] = tail
    h_end = hc_ref[...]
    for c in range(nl):
        hl_ref[0, :, c * LANE:(c + 1) * LANE] = h_end[c:c + 1, :]

    g_rnns = [_sigmoid(proj(5, s)) for s in subs]
    for s in subs:
        sg_ref[0, rsl(s), :] = _sigmoid(proj(6, s)).astype(BF16)
        hseq = jnp.concatenate(
            [jnp.concatenate([fb_refs[s][pl.ds(8 * rg * nl + c, 8, stride=nl), :] for c in range(nl)], axis=1)
             for rg in range(rs // 8)], axis=0)
        y_rnn = (hseq * gels[s]).astype(BF16)
        m = g_rnns[s] * jnp.dot(y_rnn, wpr_ref[...], preferred_element_type=F32)
        m_ref[0, rsl(s), :] = m.astype(BF16)


def _staging_slots(ref, lead):
    assert ref.shape[-1] == STAGE_COLS and ref.dtype == F32
    return [ref.at[lead + (pl.ds(r, STAGE_ROWS),)] for r in range(0, ref.shape[-2] - STAGE_ROWS + 1, STAGE_ROWS)]


def _stream_specs(nb, t, tm, d):
    nt = t // tm
    nprompt = nb * nt

    def where(s):
        c = jnp.minimum(s, nprompt - 1)
        return c // nt, c % nt
    tile = pl.BlockSpec((1, tm, d), lambda s: (*where(s), 0))
    stream = lambda n, w: pl.BlockSpec((1, n, w), lambda s: (where(s)[0], 0, 0))
    whole = lambda shape: pl.BlockSpec(tuple(shape), lambda s: (0,) * len(shape))
    return nt, nprompt, tile, stream, whole


def _front(xp, hist_p, h0_p, xs, hist_s, h0_s, weights, *, tm, nsub):
    nb, t, d = xp.shape
    bs, ts, _ = xs.shape
    kw = hist_p.shape[1] + 1
    nt, nprompt, tile, stream, whole = _stream_specs(nb, t, tm, d)
    ins = (xp, hist_p, h0_p, xs, hist_s, h0_s) + tuple(weights)
    big = (1, 4, 10)

    def body(*refs):
        ins_p, ins_s, w = refs[0:3], refs[3:6], list(refs[6:6 + len(weights)])
        o = 6 + len(weights)
        nscr = 4 + 2 * nsub
        outs_p, outs_s = refs[o:o + 7], refs[o + 7:o + 14]
        scr_p, scr_s = refs[o + 14:o + 14 + nscr], refs[o + 14 + nscr:o + 19 + nscr]
        wbuf, sem_ref = refs[o + 19 + nscr:o + 22 + nscr], refs[o + 22 + nscr]
        s = pl.program_id(0)

        @pl.when(s == 0)
        def _():
            gates = [(w[4].at[g], wbuf[1].at[g]) for g in range(w[4].shape[0])]
            slots = _staging_slots(scr_p[3], ()) + _staging_slots(scr_p[1], (0,))
            _load_as_bf16([(w[1], wbuf[0])] + gates + [(w[10], wbuf[2])], slots, sem_ref)
        for n, buf in zip(big, wbuf):
            w[n] = buf

        @pl.when(s < nprompt)
        def _():
            _front_chain_kernel(*ins_p, *w, *outs_p, *scr_p, tm=tm, d=d, nsub=nsub, first=s % nt == 0)

        @pl.when(s == nprompt)
        def _():
            _front_kernel(*ins_s, *w, *outs_s, *scr_s, bb=bs, tm=ts, d=d, first=True)

    def outs(n, rows):
        act = lambda dt: jax.ShapeDtypeStruct((n, rows, d), dt)
        return [act(BF16), act(F32), act(F32), act(BF16), act(BF16),
                jax.ShapeDtypeStruct((n, kw - 1, d), F32), jax.ShapeDtypeStruct((n, 1, d), F32)]
    out_shape = outs(nb, t) + outs(bs, ts)
    return pl.pallas_call(
        body,
        grid=(nprompt + 1,),
        in_specs=[tile, stream(kw - 1, d), stream(1, d)] + [whole(a.shape) for a in ins[3:6]]
        + [pl.BlockSpec(memory_space=pl.ANY) if n in big else whole(a.shape) for n, a in enumerate(weights)],
        out_specs=[tile] * 5 + [stream(kw - 1, d), stream(1, d)] + [whole(o.shape) for o in out_shape[7:]],
        out_shape=out_shape,
        scratch_shapes=[pltpu.VMEM((tm, d), BF16), pltpu.VMEM((1, 8 + tm, d), F32),
                        pltpu.VMEM((d // LANE, LANE), F32), pltpu.VMEM((tm, d), F32)]
        + [pltpu.VMEM((tm // nsub * d // LANE, LANE), F32)] * (2 * nsub)
        + [pltpu.VMEM((bs * ts, d), BF16), pltpu.VMEM((bs, 8 + ts, d), F32),
                        pltpu.VMEM((bs * ts, d), F32), pltpu.VMEM((bs * ts, d), F32), pltpu.VMEM((bs, 1, d), F32)]
        + [pltpu.VMEM(weights[n].shape, BF16) for n in big] + [pltpu.SemaphoreType.DMA((2 * (tm // STAGE_ROWS),))],
        compiler_params=pltpu.CompilerParams(dimension_semantics=("arbitrary",),
                                             vmem_limit_bytes=VMEM_LIMIT_BYTES),
        name="front",
    )(*ins)


def _attn_kernel(*refs, bq, bk, nq, heads, has_cache):
    if has_cache:
        q_ref, kn_ref, vn_ref, kp_ref, vp_ref, u_ref, o_ref, acc_ref, r_ref, alive_ref, kpad_ref, vpad_ref = refs
    else:
        q_ref, kn_ref, vn_ref, u_ref, o_ref, acc_ref, r_ref, alive_ref = refs
        kp_ref, vp_ref = kn_ref, vn_ref
    rh = min(bq, LANE)
    ngrp = bq // rh
    chains = [(h, g) for h in range(heads) for g in range(ngrp)]

    def head_rows(ref, start, n, h):
        if ref.shape[-1] == HEAD_DIM and heads > 1:
            return ref[0, pl.ds(start * N_HEADS + h, n, stride=N_HEADS), :]
        lanes = slice(h * HEAD_DIM, (h + 1) * HEAD_DIM)
        return ref[pl.ds(start, n), lanes] if len(ref.shape) == 2 else ref[0, pl.ds(start, n), lanes]

    def q_block(qi):
        i = pl.program_id(2) * nq + qi
        qoff = 0 if nq == 1 else pl.multiple_of(qi * bq, bq)

        def run(tiles, state):
            zs = []
            for c, k_ref, _, start, nk, _ in tiles:
                h, g = chains[c]
                qh = q_ref[0, pl.ds(qoff + g * rh, rh), h * HEAD_DIM:(h + 1) * HEAD_DIM]
                zs.append(lax.dot_general(qh, head_rows(k_ref, start, nk, h).astype(BF16), (((1,), (1,)), ((), ())),
                                          preferred_element_type=F32))
            log_betas, log_1ms = [], []
            for (_, _, _, _, _, valid), z in zip(tiles, zs):
                log_beta = jnp.minimum(z, 0.0) - jnp.log(1.0 + jnp.exp(-jnp.abs(z)))
                log_1m = log_beta - z
                log_betas.append(log_beta)
                log_1ms.append(log_1m if valid is None else jnp.where(valid, log_1m, 0.0))
            suffixes = [jnp.dot(l.astype(BF16), u_ref[0:t[4], 0:t[4]], preferred_element_type=F32)
                        for t, l in zip(tiles, log_1ms)]
            state = list(state)
            ws = []
            for n, (c, _, _, _, nk, valid) in enumerate(tiles):
                arg = log_betas[n] + suffixes[n]
                rs = jnp.broadcast_to(jnp.sum(log_1ms[n], axis=-1, keepdims=True), (rh, LANE))
                if state[c] is None:
                    state[c] = (rs, None)
                else:
                    r, acc = state[c]
                    arg = arg + (r if nk == LANE else jnp.concatenate([r] * (nk // LANE), axis=1))
                    state[c] = (r + rs, acc)
                w = jnp.exp(arg)
                ws.append((w if valid is None else jnp.where(valid, w, 0.0)).astype(BF16))
            for n, (c, _, v_ref, start, nk, _) in enumerate(tiles):
                pv = jnp.dot(ws[n], head_rows(v_ref, start, nk, chains[c][0]).astype(BF16),
                             preferred_element_type=F32)
                r, acc = state[c]
                state[c] = (r, pv if acc is None else acc + pv)
            return state

        def save(state):
            for c, entry in enumerate(state):
                if entry is not None:
                    r_ref[c], acc_ref[c] = entry

        def load(only=None):
            return [(r_ref[c], acc_ref[c]) if only is None or only(g) else None for c, (_, g) in enumerate(chains)]

        def live():
            return jnp.max(r_ref[...]) > SUFFIX_FLOOR

        def first(tiles):
            state = run(tiles, [None] * len(chains))
            save(state)
            top = functools.reduce(jnp.maximum, [r for r, _ in state])
            alive_ref[0] = (jnp.max(top) > SUFFIX_FLOOR).astype(jnp.int32)

        if has_cache:
            tn = kn_ref.shape[1]
            for pad_ref, src_ref in ((kpad_ref, kn_ref), (vpad_ref, vn_ref)):
                pad_ref[...] = jnp.zeros_like(pad_ref)
                pad_ref[0:tn, :] = src_ref[0]
            kd_ref, vd_ref, dstart = kpad_ref, vpad_ref, 0
        else:
            kd_ref, vd_ref, dstart = kn_ref, vn_ref, pl.multiple_of(i * bq, bq)
        diag = []
        for c, (h, g) in enumerate(chains):
            nk = -(-((g + 1) * rh) // LANE) * LANE
            row = lax.broadcasted_iota(jnp.int32, (rh, nk), 0) + g * rh
            col = lax.broadcasted_iota(jnp.int32, (rh, nk), 1)
            diag.append((c, kd_ref, vd_ref, dstart, nk, col < row))

        def past(j):
            start = j * bk if isinstance(j, int) else pl.multiple_of(j * bk, bk)
            return [(c, kp_ref, vp_ref, start, bk, None) for c in range(len(chains))]

        if has_cache:
            j0 = kp_ref.shape[1] // N_HEADS // bk - 1
            first(diag + past(j0))
            j1 = jnp.int32(j0 - 1)
        else:
            def earlier(lo, n):
                return pl.multiple_of(dstart - bk + lo, LANE), n

            @pl.when(i > 0)
            def _():
                first(diag + [(c, kp_ref, vp_ref) + earlier(g * rh, bk - g * rh) + (None,)
                              for c, (_, g) in enumerate(chains)])

            @pl.when(i == 0)
            def _():
                first(diag)
                alive_ref[0] = 0
            j1 = i - 2

        @pl.when(alive_ref[0] > 0)
        def _():
            if not has_cache and ngrp > 1:
                save(run([(c, kp_ref, vp_ref) + earlier(0, g * rh) + (None,)
                          for c, (_, g) in enumerate(chains) if g > 0], load(lambda g: g > 0)))

            def cond(c):
                return jnp.logical_and(c[0] >= 0, c[1])

            def body(c):
                save(run(past(c[0]), load()))
                return c[0] - 1, live()
            lax.while_loop(cond, body, (j1, live()))

        for c, (h, g) in enumerate(chains):
            o_ref[0, pl.ds(qoff + g * rh, rh), h * HEAD_DIM:(h + 1) * HEAD_DIM] = acc_ref[c].astype(o_ref.dtype)

    if nq == 1:
        q_block(0)
    else:
        def q_step(qi, carry):
            q_block(qi)
            return carry
        lax.fori_loop(0, nq, q_step, 0)


def _attention(q, kn, vn, kc=None, vc=None, *, bq, bk, nq, heads):
    nb, t, d = q.shape
    hw = heads * HEAD_DIM
    has_cache = kc is not None
    assert has_cache or (bk == bq and bq % LANE == 0)
    assert not has_cache or (heads == N_HEADS and kc.shape[2] == HEAD_DIM and t == bq <= LANE and nq == 1)
    rh = min(bq, LANE)
    nchain = heads * (bq // rh)
    grid = (nb, d // hw, t // (bq * nq))
    qspec = pl.BlockSpec((1, bq * nq, hw), lambda b, g, i: (b, i, g))
    kvspec = lambda a: pl.BlockSpec((1, a.shape[1], hw), lambda b, g, i: (b, 0, g))
    cachespec = lambda a: pl.BlockSpec((1,) + a.shape[1:], lambda b, g, i: (b, 0, 0))
    u = jnp.tril(jnp.ones((bk, bk), F32), -1).astype(BF16)
    kv = [kn, vn] + ([kc, vc] if has_cache else [])
    scratch = [pltpu.VMEM((nchain, rh, HEAD_DIM), F32), pltpu.VMEM((nchain, rh, LANE), F32),
               pltpu.SMEM((1,), jnp.int32)]
    if has_cache:
        scratch += [pltpu.VMEM((LANE, hw), F32), pltpu.VMEM((LANE, hw), F32)]
    return pl.pallas_call(
        functools.partial(_attn_kernel, bq=bq, bk=bk, nq=nq, heads=heads, has_cache=has_cache),
        grid=grid,
        in_specs=[qspec, kvspec(kn), kvspec(vn)] + [cachespec(a) for a in kv[2:]]
        + [pl.BlockSpec(u.shape, lambda b, g, i: (0, 0))],
        out_specs=qspec,
        out_shape=jax.ShapeDtypeStruct((nb, t, d), BF16),
        scratch_shapes=scratch,
        compiler_params=pltpu.CompilerParams(dimension_semantics=("arbitrary", "arbitrary", "arbitrary"),
                                             vmem_limit_bytes=VMEM_LIMIT_BYTES),
        name="attn_cache" if has_cache else "attn",
    )(q, *kv, u)


def _back_kernel(x_ref, m_ref, sg_ref, ya_ref, fh_ref, wpa_ref, wo_ref, ln2_ref, wup_ref, fw_ref, fb_ref, wdn_ref,
                 y_ref, ft_ref, gp_ref, u2_ref, val_ref, *, bb, tm, d, dff, nsub, first):
    rows = bb * tm
    rs = rows // nsub
    kw = fw_ref.shape[0]
    pad = 8
    nchunk = dff // d
    subs = range(nsub)

    def load(ref, s):
        return ref[0, s * rs:(s + 1) * rs, :] if bb == 1 else ref[...].reshape(rows, d)

    def rsl(s):
        return slice(s * rs, (s + 1) * rs)

    @_when(first)
    def _():
        for c in range(nchunk):
            gp_ref[c, :, pad - (kw - 1):pad, :] = fh_ref[:, :, c * d:(c + 1) * d]

    atts = [jnp.dot(load(ya_ref, s), wpa_ref[...], preferred_element_type=F32) for s in subs]
    merged = [(load(m_ref, s).astype(F32) + load(sg_ref, s).astype(F32) * atts[s]).astype(BF16) for s in subs]
    outs = [jnp.dot(merged[s], wo_ref[...], preferred_element_type=F32) for s in subs]
    for s in subs:
        x1 = load(x_ref, s) + outs[s]
        if bb == 1:
            y_ref[0, rsl(s), :] = x1
        else:
            y_ref[...] = x1.reshape(bb, tm, d)
        u2_ref[rsl(s), :] = ((x1 * _rms_scale(x1)) * ln2_ref[...]).astype(BF16)

    def up(c):
        for s in subs:
            gp = jnp.dot(u2_ref[rsl(s), :], wup_ref[:, c * d:(c + 1) * d], preferred_element_type=F32)
            if bb == 1:
                gp_ref[c, 0, pad + s * rs:pad + (s + 1) * rs, :] = gp
            else:
                for b in range(bb):
                    gp_ref[c, b, pad:pad + tm, :] = gp[b * tm:(b + 1) * tm]
            val_ref[c % 2, rsl(s), :] = jnp.dot(u2_ref[rsl(s), :], wup_ref[:, dff + c * d:dff + (c + 1) * d],
                                                preferred_element_type=F32)

    def conv(c, b, start, n):
        cs = slice(c * d, (c + 1) * d)
        off = start - (kw - 1)
        g = gp_ref[c, b, off:off + n, :] * fw_ref[0:1, cs]
        for j in range(1, kw):
            g = g + gp_ref[c, b, off + j:off + j + n, :] * fw_ref[j:j + 1, cs]
        return g + fb_ref[:, cs]

    up(0)
    for c in range(nchunk):
        if c + 1 < nchunk:
            up(c + 1)
        cs = slice(c * d, (c + 1) * d)
        for s in subs:
            if bb == 1:
                gc = conv(c, 0, pad + s * rs, rs)
            else:
                gc = jnp.concatenate([conv(c, b, pad, tm) for b in range(bb)], axis=0)
            act = (_gelu_tanh(gc) * val_ref[c % 2, rsl(s), :]).astype(BF16)
            part = jnp.dot(act, wdn_ref[c * d:(c + 1) * d, :], preferred_element_type=F32)
            if bb == 1:
                y_ref[0, rsl(s), :] += part
            else:
                y_ref[...] += part.reshape(bb, tm, d)
        for b in range(bb):
            tail = gp_ref[c, b, pad + tm - (kw - 1):pad + tm, :]
            ft_ref[b, :, cs] = tail
            gp_ref[c, b, pad - (kw - 1):pad, :] = tail


def _back(prompt, sample, weights, *, tm, nsub):
    nb, t, d = prompt[0].shape
    bs, ts, _ = sample[0].shape
    dff = weights[-1].shape[0]
    kw = weights[4].shape[0]
    nt, nprompt, tile, stream, whole = _stream_specs(nb, t, tm, d)
    ins = tuple(prompt) + tuple(sample) + tuple(weights)
    big = (0, 1, 3, 6)

    def body(*refs):
        ins_p, ins_s, w = refs[0:5], refs[5:10], list(refs[10:10 + len(weights)])
        o = 10 + len(weights)
        outs_p, outs_s, scr_p, scr_s = refs[o:o + 2], refs[o + 2:o + 4], refs[o + 4:o + 7], refs[o + 7:o + 10]
        wbuf, sem_ref = refs[o + 10:o + 14], refs[o + 14]
        s = pl.program_id(0)

        @pl.when(s == 0)
        def _():
            slots = [v for c in range(dff // d) for v in _staging_slots(scr_p[0], (c, 0))]
            _load_as_bf16([(w[n], buf) for n, buf in zip(big, wbuf)], slots, sem_ref)
        for n, buf in zip(big, wbuf):
            w[n] = buf

        @pl.when(s < nprompt)
        def _():
            _back_kernel(*ins_p, *w, *outs_p, *scr_p, bb=1, tm=tm, d=d, dff=dff, nsub=nsub, first=s % nt == 0)

        @pl.when(s == nprompt)
        def _():
            _back_kernel(*ins_s, *w, *outs_s, *scr_s, bb=bs, tm=ts, d=d, dff=dff, nsub=1, first=True)

    out_shape = [jax.ShapeDtypeStruct((nb, t, d), F32), jax.ShapeDtypeStruct((nb, kw - 1, dff), F32),
                 jax.ShapeDtypeStruct((bs, ts, d), F32), jax.ShapeDtypeStruct((bs, kw - 1, dff), F32)]
    return pl.pallas_call(
        body,
        grid=(nprompt + 1,),
        in_specs=[tile] * 4 + [stream(kw - 1, dff)] + [whole(a.shape) for a in ins[5:10]]
        + [pl.BlockSpec(memory_space=pl.ANY) if n in big else whole(a.shape) for n, a in enumerate(weights)],
        out_specs=[tile, stream(kw - 1, dff)] + [whole(o.shape) for o in out_shape[2:]],
        out_shape=out_shape,
        scratch_shapes=[pltpu.VMEM((dff // d, 1, 8 + tm, d), F32), pltpu.VMEM((tm, d), BF16),
                        pltpu.VMEM((2, tm, d), F32),
                        pltpu.VMEM((dff // d, bs, 8 + ts, d), F32), pltpu.VMEM((bs * ts, d), BF16),
                        pltpu.VMEM((2, bs * ts, d), F32)]
        + [pltpu.VMEM(weights[n].shape, BF16) for n in big]
        + [pltpu.SemaphoreType.DMA((dff // d * (tm // STAGE_ROWS),))],
        compiler_params=pltpu.CompilerParams(dimension_semantics=("arbitrary",),
                                             vmem_limit_bytes=VMEM_LIMIT_BYTES),
        name="back",
    )(*ins)


def _dense_gate_blocks(w):
    nblk, c, _ = w.shape
    per = GATE_GROUP // c
    w4 = w.reshape(nblk // per, per, c, c)
    eye = jnp.eye(per, dtype=w.dtype)
    return jnp.einsum('gncd,nm->gncmd', w4, eye).reshape(nblk // per, GATE_GROUP, GATE_GROUP)


def kernel(x_prompt, x_sample, cache_k, cache_v, state_rnn_conv, state_rnn_h, state_ffn_conv, ln1, w_in, rnn_conv_w, rnn_conv_b, lru_wa, lru_ba, lru_wx, lru_bx, lru_lambda, q_norm_g, k_norm_g, w_proj_rnn, w_proj_attn, w_out, ln2, w_up, ffn_conv_w, ffn_conv_b, w_down):
    depth = ln1.shape[0]
    assert depth == 1
    bp, tp, d = x_prompt.shape
    bs, ts, _ = x_sample.shape
    dff = w_down.shape[1]
    row = lambda a: a[0].reshape(1, -1)
    wg = jnp.concatenate([_dense_gate_blocks(lru_wa[0]), _dense_gate_blocks(lru_wx[0])], axis=-1)
    front_w = (row(ln1), w_in[0], rnn_conv_w[0], row(rnn_conv_b), wg, row(lru_ba), row(lru_bx),
               row(lru_lambda), row(q_norm_g), row(k_norm_g), w_proj_rnn[0])
    back_w = (w_proj_attn[0], w_out[0], row(ln2), w_up[0],
              ffn_conv_w[0], row(ffn_conv_b), w_down[0])
    kw = rnn_conv_w.shape[1]
    fkw = ffn_conv_w.shape[1]

    zc = jnp.zeros((bp, kw - 1, d), F32)
    zh = jnp.zeros((bp, 1, d), F32)
    zf = jnp.zeros((bp, fkw - 1, dff), F32)
    (qp, kp, vp, mp, sgp, rcp, hp, qs, ks, vs, ms, sgs, rcs, hs) = _front(
        x_prompt, zc, zh, x_sample, state_rnn_conv[0], state_rnn_h[0].reshape(bs, 1, d), front_w, tm=512, nsub=2)
    yap = _attention(qp, kp, vp, bq=256, bk=256, nq=4, heads=4)
    yas = _attention(qs, ks, vs, cache_k[0].reshape(bs, -1, HEAD_DIM), cache_v[0].reshape(bs, -1, HEAD_DIM),
                     bq=ts, bk=256, nq=1, heads=N_HEADS)
    yp, fcp, ys, fcs = _back((x_prompt, mp, sgp, yap, zf), (x_sample, ms, sgs, yas, state_ffn_conv[0]), back_w,
                             tm=512, nsub=2)
    hd = lambda a: a.reshape(1, a.shape[0], a.shape[1], N_HEADS, HEAD_DIM)
    return (yp, ys, hd(kp), hd(vp), rcp[None], hp.reshape(1, bp, d), fcp[None],
            hd(ks), hd(vs), rcs[None], hs.reshape(1, bs, d), fcs[None])
```

```python
import functools
import math

import jax
import jax.numpy as jnp
from jax import lax
from jax.experimental import pallas as pl
from jax.experimental.pallas import tpu as pltpu

F32 = jnp.float32
BF16 = jnp.bfloat16

NORM_EPS = 1e-6
LRU_C = 8.0
N_HEADS = 8
HEAD_DIM = 128
N_LRU_BLOCKS = 16
LANE = 128
GATE_GROUP = 256
VMEM_LIMIT_BYTES = 56 * 1024 * 1024
SUFFIX_FLOOR = -104.0
STAGE_ROWS, STAGE_COLS = 256, 1024


def _gelu_tanh(x):
    c = math.sqrt(2.0 / math.pi)
    hx = 0.5 * x
    return hx + hx * jnp.tanh(x * (c + (0.044715 * c) * (x * x)))


def _sqrt_nonneg(x):
    return jnp.where(x > 0.0, x * lax.rsqrt(x), 0.0)


def _sigmoid(x):
    return 0.5 * jnp.tanh(0.5 * x) + 0.5


def _load_as_bf16(pairs, slots, sem_ref):
    nslot = len(slots)
    tiles = []
    for src, dst in pairs:
        k, n = src.shape
        tr, tc = min(k, STAGE_ROWS), min(n, STAGE_COLS)
        tiles += [(src, dst, r0, c0, tr, tc) for r0 in range(0, k, tr) for c0 in range(0, n, tc)]

    def copy(i):
        src, _, r0, c0, tr, tc = tiles[i]
        return pltpu.make_async_copy(src.at[r0:r0 + tr, c0:c0 + tc], slots[i % nslot].at[0:tr, 0:tc],
                                     sem_ref.at[i % nslot])
    for i in range(min(nslot - 1, len(tiles))):
        copy(i).start()
    for i, (_, dst, r0, c0, tr, tc) in enumerate(tiles):
        if i + nslot - 1 < len(tiles):
            copy(i + nslot - 1).start()
        copy(i).wait()
        dst[r0:r0 + tr, c0:c0 + tc] = slots[i % nslot][0:tr, 0:tc].astype(BF16)


def _when(cond):
    if isinstance(cond, bool):
        return (lambda f: f()) if cond else (lambda f: None)
    return pl.when(cond)


def _softplus(x):
    return jnp.maximum(x, 0.0) + jnp.log1p(jnp.exp(-jnp.abs(x)))


def _rms_scale(x):
    return lax.rsqrt(jnp.mean(x * x, axis=-1, keepdims=True) + NORM_EPS)


def _head_rmsnorm(x, g):
    outs = []
    for h in range(N_HEADS):
        xh = x[:, h * HEAD_DIM:(h + 1) * HEAD_DIM]
        outs.append(xh * _rms_scale(xh) * g)
    return outs


def _front_kernel(x_ref, hist_ref, h0_ref, ln1_ref, w_in_ref, cw_ref, cb_ref, wg_ref, ba_ref, bx_ref,
                  lam_ref, qg_ref, kg_ref, wpr_ref,
                  q_ref, k_ref, v_ref, m_ref, sg_ref, tail_ref, hl_ref,
                  u_ref, xp_ref, a_ref, b_ref, hc_ref, *, bb, tm, d, first):
    rows = bb * tm
    kw = cw_ref.shape[0]
    pad = 8

    @_when(first)
    def _():
        xp_ref[:, pad - (kw - 1):pad, :] = hist_ref[...]
        hc_ref[...] = h0_ref[...]

    x = x_ref[...].reshape(rows, d)
    u_ref[...] = ((x * _rms_scale(x)) * ln1_ref[...]).astype(BF16)

    def proj(c):
        return jnp.dot(u_ref[...], w_in_ref[:, c * d:(c + 1) * d], preferred_element_type=F32)

    xr = proj(0)
    for b in range(bb):
        xp_ref[b, pad:pad + tm, :] = xr[b * tm:(b + 1) * tm]
    xcs = []
    for b in range(bb):
        off = pad - (kw - 1)
        y = xp_ref[b, off:off + tm, :] * cw_ref[0:1, :]
        for j in range(1, kw):
            y = y + xp_ref[b, off + j:off + j + tm, :] * cw_ref[j:j + 1, :]
        xcs.append(y + cb_ref[...])
        tail = xp_ref[b, pad + tm - (kw - 1):pad + tm, :]
        tail_ref[b] = tail
        xp_ref[b, pad - (kw - 1):pad, :] = tail
    xc = xcs[0] if bb == 1 else jnp.concatenate(xcs, axis=0)

    sp_lam = _softplus(-lam_ref[...])
    for g in range(d // GATE_GROUP):
        sl = slice(g * GATE_GROUP, (g + 1) * GATE_GROUP)
        xg = xc[:, sl]
        pre = jnp.dot(xg.astype(BF16), wg_ref[g], preferred_element_type=F32)
        r = _sigmoid(pre[:, :GATE_GROUP] + ba_ref[:, sl])
        ig = _sigmoid(pre[:, GATE_GROUP:] + bx_ref[:, sl])
        log_a = (-LRU_C) * r * sp_lam[:, sl]
        a = jnp.exp(log_a)
        one_m_a2 = -jnp.tanh(log_a) * (a * a + 1.0)
        a_ref[:, sl] = a
        b_ref[:, sl] = _sqrt_nonneg(one_m_a2) * (ig * xg)

    for b in range(bb):
        def step(t, h, b=b):
            r = b * tm + t
            h = a_ref[pl.ds(r, 1), :] * h + b_ref[pl.ds(r, 1), :]
            b_ref[pl.ds(r, 1), :] = h
            return h
        h_end = lax.fori_loop(0, tm, step, hc_ref[b], unroll=8)
        hc_ref[b] = h_end
        hl_ref[b] = h_end

    y_rnn = (b_ref[...] * _gelu_tanh(proj(1))).astype(BF16)
    m = _sigmoid(proj(5)) * jnp.dot(y_rnn, wpr_ref[...], preferred_element_type=F32)
    m_ref[...] = m.astype(BF16).reshape(bb, tm, d)

    qn = _head_rmsnorm(proj(2), qg_ref[...])
    for h in range(N_HEADS):
        q_ref[:, :, h * HEAD_DIM:(h + 1) * HEAD_DIM] = (qn[h] * HEAD_DIM ** -0.5).astype(BF16).reshape(
            bb, tm, HEAD_DIM)
    kn = _head_rmsnorm(proj(3), kg_ref[...])
    for h in range(N_HEADS):
        k_ref[:, :, h * HEAD_DIM:(h + 1) * HEAD_DIM] = kn[h].reshape(bb, tm, HEAD_DIM)
    v_ref[...] = proj(4).reshape(bb, tm, d)
    sg_ref[...] = _sigmoid(proj(6)).astype(BF16).reshape(bb, tm, d)


def _front_chain_kernel(x_ref, hist_ref, h0_ref, ln1_ref, w_in_ref, cw_ref, cb_ref, wg_ref, ba_ref, bx_ref,
                        lam_ref, qg_ref, kg_ref, wpr_ref,
                        q_ref, k_ref, v_ref, m_ref, sg_ref, tail_ref, hl_ref,
                        u_ref, xp_ref, hc_ref, xc_ref, *fab_refs, tm, d, nsub, first):
    fa_refs, fb_refs = fab_refs[0::2], fab_refs[1::2]
    rs = tm // nsub
    kw = cw_ref.shape[0]
    pad = 8
    nl = d // LANE
    gl = GATE_GROUP // LANE
    subs = range(nsub)

    def rsl(s):
        return slice(s * rs, (s + 1) * rs)

    @_when(first)
    def _():
        xp_ref[0, pad - (kw - 1):pad, :] = hist_ref[0]
        for c in range(nl):
            hc_ref[c:c + 1, :] = h0_ref[0, :, c * LANE:(c + 1) * LANE]

    for s in subs:
        x = x_ref[0, rsl(s), :]
        u_ref[rsl(s), :] = ((x * _rms_scale(x)) * ln1_ref[...]).astype(BF16)

    def proj(c, s):
        return jnp.dot(u_ref[rsl(s), :], w_in_ref[:, c * d:(c + 1) * d], preferred_element_type=F32)

    for s in subs:
        xp_ref[0, pad + s * rs:pad + (s + 1) * rs, :] = proj(0, s)

    ngrp = d // GATE_GROUP
    sp_lam = _softplus(-lam_ref[...])
    gels = [None] * nsub

    def conv(s):
        off = pad + s * rs - (kw - 1)
        xc = xp_ref[0, off:off + rs, :] * cw_ref[0:1, :]
        for j in range(1, kw):
            xc = xc + xp_ref[0, off + j:off + j + rs, :] * cw_ref[j:j + 1, :]
        xc_ref[rsl(s), :] = xc + cb_ref[...]

    def gate(s, g):
        sl = slice(g * GATE_GROUP, (g + 1) * GATE_GROUP)
        xg = xc_ref[rsl(s), sl]
        pre = jnp.dot(xg.astype(BF16), wg_ref[g], preferred_element_type=F32)
        r = _sigmoid(pre[:, :GATE_GROUP] + ba_ref[:, sl])
        ig = _sigmoid(pre[:, GATE_GROUP:] + bx_ref[:, sl])
        log_a = (-LRU_C) * r * sp_lam[:, sl]
        a = jnp.exp(log_a)
        bv = _sqrt_nonneg(-jnp.tanh(log_a) * (a * a + 1.0)) * (ig * xg)
        for rg in range(rs // 8):
            for cl in range(gl):
                dst = pl.ds(8 * rg * nl + g * gl + cl, 8, stride=nl)
                fa_refs[s][dst, :] = a[8 * rg:8 * rg + 8, cl * LANE:(cl + 1) * LANE]
                fb_refs[s][dst, :] = bv[8 * rg:8 * rg + 8, cl * LANE:(cl + 1) * LANE]

    def scan(s):
        h = hc_ref[...]
        for t in range(rs):
            h = fa_refs[s][t * nl:(t + 1) * nl, :] * h + fb_refs[s][t * nl:(t + 1) * nl, :]
            fb_refs[s][t * nl:(t + 1) * nl, :] = h
        hc_ref[...] = h

    def keys(s):
        kn = _head_rmsnorm(proj(3, s), kg_ref[...])
        for h in range(N_HEADS):
            k_ref[0, rsl(s), h * HEAD_DIM:(h + 1) * HEAD_DIM] = kn[h]

    def values(s):
        v_ref[0, rsl(s), :] = proj(4, s)

    def queries(s):
        qn = _head_rmsnorm(proj(2, s), qg_ref[...])
        for h in range(N_HEADS):
            q_ref[0, rsl(s), h * HEAD_DIM:(h + 1) * HEAD_DIM] = (qn[h] * HEAD_DIM ** -0.5).astype(BF16)

    def gelu_gate(s):
        gels[s] = _gelu_tanh(proj(1, s))

    wide = [f for s in subs for f in (keys, values, queries)] + [gelu_gate] * nsub
    wide_args = [s for s in subs for _ in range(3)] + list(subs)
    small = [(s, g) for s in subs for g in range(ngrp)]
    conv(0)
    for n in range(max(len(wide), len(small))):
        if n < len(wide):
            wide[n](wide_args[n])
        if n < len(small):
            s, g = small[n]
            if g == ngrp - 1 and s + 1 < nsub:
                conv(s + 1)
            gate(s, g)
            if g == ngrp - 1:
                scan(s)
    tail = xp_ref[0, pad + tm - (kw - 1):pad + tm, :]
    tail_ref[0] = tail
    xp_ref[0, pad - (kw - 1):pad, :] = tail
    h_end = hc_ref[...]
    for c in range(nl):
        hl_ref[0, :, c * LANE:(c + 1) * LANE] = h_end[c:c + 1, :]

    g_rnns = [_sigmoid(proj(5, s)) for s in subs]
    for s in subs:
        sg_ref[0, rsl(s), :] = _sigmoid(proj(6, s)).astype(BF16)
        hseq = jnp.concatenate(
            [jnp.concatenate([fb_refs[s][pl.ds(8 * rg * nl + c, 8, stride=nl), :] for c in range(nl)], axis=1)
             for rg in range(rs // 8)], axis=0)
        y_rnn = (hseq * gels[s]).astype(BF16)
        m = g_rnns[s] * jnp.dot(y_rnn, wpr_ref[...], preferred_element_type=F32)
        m_ref[0, rsl(s), :] = m.astype(BF16)


def _staging_slots(ref, lead):
    assert ref.shape[-1] == STAGE_COLS and ref.dtype == F32
    return [ref.at[lead + (pl.ds(r, STAGE_ROWS),)] for r in range(0, ref.shape[-2] - STAGE_ROWS + 1, STAGE_ROWS)]


def _stream_specs(nb, t, tm, d):
    nt = t // tm
    nprompt = nb * nt

    def where(s):
        c = jnp.minimum(s, nprompt - 1)
        return c // nt, c % nt
    tile = pl.BlockSpec((1, tm, d), lambda s: (*where(s), 0))
    stream = lambda n, w: pl.BlockSpec((1, n, w), lambda s: (where(s)[0], 0, 0))
    whole = lambda shape: pl.BlockSpec(tuple(shape), lambda s: (0,) * len(shape))
    return nt, nprompt, tile, stream, whole


def _front(xp, hist_p, h0_p, xs, hist_s, h0_s, weights, *, tm, nsub):
    nb, t, d = xp.shape
    bs, ts, _ = xs.shape
    kw = hist_p.shape[1] + 1
    nt, nprompt, tile, stream, whole = _stream_specs(nb, t, tm, d)
    ins = (xp, hist_p, h0_p, xs, hist_s, h0_s) + tuple(weights)
    big = (1, 4, 10)

    def body(*refs):
        ins_p, ins_s, w = refs[0:3], refs[3:6], list(refs[6:6 + len(weights)])
        o = 6 + len(weights)
        nscr = 4 + 2 * nsub
        outs_p, outs_s = refs[o:o + 7], refs[o + 7:o + 14]
        scr_p, scr_s = refs[o + 14:o + 14 + nscr], refs[o + 14 + nscr:o + 19 + nscr]
        wbuf, sem_ref = refs[o + 19 + nscr:o + 22 + nscr], refs[o + 22 + nscr]
        s = pl.program_id(0)

        @pl.when(s == 0)
        def _():
            gates = [(w[4].at[g], wbuf[1].at[g]) for g in range(w[4].shape[0])]
            slots = _staging_slots(scr_p[3], ()) + _staging_slots(scr_p[1], (0,))
            _load_as_bf16([(w[1], wbuf[0])] + gates + [(w[10], wbuf[2])], slots, sem_ref)
        for n, buf in zip(big, wbuf):
            w[n] = buf

        @pl.when(s < nprompt)
        def _():
            _front_chain_kernel(*ins_p, *w, *outs_p, *scr_p, tm=tm, d=d, nsub=nsub, first=s % nt == 0)

        @pl.when(s == nprompt)
        def _():
            _front_kernel(*ins_s, *w, *outs_s, *scr_s, bb=bs, tm=ts, d=d, first=True)

    def outs(n, rows):
        act = lambda dt: jax.ShapeDtypeStruct((n, rows, d), dt)
        return [act(BF16), act(F32), act(F32), act(BF16), act(BF16),
                jax.ShapeDtypeStruct((n, kw - 1, d), F32), jax.ShapeDtypeStruct((n, 1, d), F32)]
    out_shape = outs(nb, t) + outs(bs, ts)
    return pl.pallas_call(
        body,
        grid=(nprompt + 1,),
        in_specs=[tile, stream(kw - 1, d), stream(1, d)] + [whole(a.shape) for a in ins[3:6]]
        + [pl.BlockSpec(memory_space=pl.ANY) if n in big else whole(a.shape) for n, a in enumerate(weights)],
        out_specs=[tile] * 5 + [stream(kw - 1, d), stream(1, d)] + [whole(o.shape) for o in out_shape[7:]],
        out_shape=out_shape,
        scratch_shapes=[pltpu.VMEM((tm, d), BF16), pltpu.VMEM((1, 8 + tm, d), F32),
                        pltpu.VMEM((d // LANE, LANE), F32), pltpu.VMEM((tm, d), F32)]
        + [pltpu.VMEM((tm // nsub * d // LANE, LANE), F32)] * (2 * nsub)
        + [pltpu.VMEM((bs * ts, d), BF16), pltpu.VMEM((bs, 8 + ts, d), F32),
                        pltpu.VMEM((bs * ts, d), F32), pltpu.VMEM((bs * ts, d), F32), pltpu.VMEM((bs, 1, d), F32)]
        + [pltpu.VMEM(weights[n].shape, BF16) for n in big] + [pltpu.SemaphoreType.DMA((2 * (tm // STAGE_ROWS),))],
        compiler_params=pltpu.CompilerParams(dimension_semantics=("arbitrary",),
                                             vmem_limit_bytes=VMEM_LIMIT_BYTES),
        name="front",
    )(*ins)


def _attn_kernel(*refs, bq, bk, nq, heads, has_cache):
    if has_cache:
        q_ref, kn_ref, vn_ref, kp_ref, vp_ref, u_ref, o_ref, acc_ref, r_ref, alive_ref, kpad_ref, vpad_ref = refs
    else:
        q_ref, kn_ref, vn_ref, u_ref, o_ref, acc_ref, r_ref, alive_ref = refs
        kp_ref, vp_ref = kn_ref, vn_ref
    rh = min(bq, LANE)
    ngrp = bq // rh
    chains = [(h, g) for h in range(heads) for g in range(ngrp)]

    def head_rows(ref, start, n, h):
        if ref.shape[-1] == HEAD_DIM and heads > 1:
            return ref[0, pl.ds(start * N_HEADS + h, n, stride=N_HEADS), :]
        lanes = slice(h * HEAD_DIM, (h + 1) * HEAD_DIM)
        return ref[pl.ds(start, n), lanes] if len(ref.shape) == 2 else ref[0, pl.ds(start, n), lanes]

    def q_block(qi):
        i = pl.program_id(2) * nq + qi
        qoff = 0 if nq == 1 else pl.multiple_of(qi * bq, bq)

        def run(tiles, state):
            zs = []
            for c, k_ref, _, start, nk, _ in tiles:
                h, g = chains[c]
                qh = q_ref[0, pl.ds(qoff + g * rh, rh), h * HEAD_DIM:(h + 1) * HEAD_DIM]
                zs.append(lax.dot_general(qh, head_rows(k_ref, start, nk, h).astype(BF16), (((1,), (1,)), ((), ())),
                                          preferred_element_type=F32))
            log_betas, log_1ms = [], []
            for (_, _, _, _, _, valid), z in zip(tiles, zs):
                log_beta = jnp.minimum(z, 0.0) - jnp.log(1.0 + jnp.exp(-jnp.abs(z)))
                log_1m = log_beta - z
                log_betas.append(log_beta)
                log_1ms.append(log_1m if valid is None else jnp.where(valid, log_1m, 0.0))
            suffixes = [jnp.dot(l.astype(BF16), u_ref[0:t[4], 0:t[4]], preferred_element_type=F32)
                        for t, l in zip(tiles, log_1ms)]
            state = list(state)
            ws = []
            for n, (c, _, _, _, nk, valid) in enumerate(tiles):
                arg = log_betas[n] + suffixes[n]
                rs = jnp.broadcast_to(jnp.sum(log_1ms[n], axis=-1, keepdims=True), (rh, LANE))
                if state[c] is None:
                    state[c] = (rs, None)
                else:
                    r, acc = state[c]
                    arg = arg + (r if nk == LANE else jnp.concatenate([r] * (nk // LANE), axis=1))
                    state[c] = (r + rs, acc)
                w = jnp.exp(arg)
                ws.append((w if valid is None else jnp.where(valid, w, 0.0)).astype(BF16))
            for n, (c, _, v_ref, start, nk, _) in enumerate(tiles):
                pv = jnp.dot(ws[n], head_rows(v_ref, start, nk, chains[c][0]).astype(BF16),
                             preferred_element_type=F32)
                r, acc = state[c]
                state[c] = (r, pv if acc is None else acc + pv)
            return state

        def save(state):
            for c, entry in enumerate(state):
                if entry is not None:
                    r_ref[c], acc_ref[c] = entry

        def load(only=None):
            return [(r_ref[c], acc_ref[c]) if only is None or only(g) else None for c, (_, g) in enumerate(chains)]

        def live():
            return jnp.max(r_ref[...]) > SUFFIX_FLOOR

        def first(tiles):
            state = run(tiles, [None] * len(chains))
            save(state)
            top = functools.reduce(jnp.maximum, [r for r, _ in state])
            alive_ref[0] = (jnp.max(top) > SUFFIX_FLOOR).astype(jnp.int32)

        if has_cache:
            tn = kn_ref.shape[1]
            for pad_ref, src_ref in ((kpad_ref, kn_ref), (vpad_ref, vn_ref)):
                pad_ref[...] = jnp.zeros_like(pad_ref)
                pad_ref[0:tn, :] = src_ref[0]
            kd_ref, vd_ref, dstart = kpad_ref, vpad_ref, 0
        else:
            kd_ref, vd_ref, dstart = kn_ref, vn_ref, pl.multiple_of(i * bq, bq)
        diag = []
        for c, (h, g) in enumerate(chains):
            nk = -(-((g + 1) * rh) // LANE) * LANE
            row = lax.broadcasted_iota(jnp.int32, (rh, nk), 0) + g * rh
            col = lax.broadcasted_iota(jnp.int32, (rh, nk), 1)
            diag.append((c, kd_ref, vd_ref, dstart, nk, col < row))

        def past(j):
            start = j * bk if isinstance(j, int) else pl.multiple_of(j * bk, bk)
            return [(c, kp_ref, vp_ref, start, bk, None) for c in range(len(chains))]

        if has_cache:
            j0 = kp_ref.shape[1] // N_HEADS // bk - 1
            first(diag + past(j0))
            j1 = jnp.int32(j0 - 1)
        else:
            def earlier(lo, n):
                return pl.multiple_of(dstart - bk + lo, LANE), n

            @pl.when(i > 0)
            def _():
                first(diag + [(c, kp_ref, vp_ref) + earlier(g * rh, bk - g * rh) + (None,)
                              for c, (_, g) in enumerate(chains)])

            @pl.when(i == 0)
            def _():
                first(diag)
                alive_ref[0] = 0
            j1 = i - 2

        @pl.when(alive_ref[0] > 0)
        def _():
            if not has_cache and ngrp > 1:
                save(run([(c, kp_ref, vp_ref) + earlier(0, g * rh) + (None,)
                          for c, (_, g) in enumerate(chains) if g > 0], load(lambda g: g > 0)))

            def cond(c):
                return jnp.logical_and(c[0] >= 0, c[1])

            def body(c):
                save(run(past(c[0]), load()))
                return c[0] - 1, live()
            lax.while_loop(cond, body, (j1, live()))

        for c, (h, g) in enumerate(chains):
            o_ref[0, pl.ds(qoff + g * rh, rh), h * HEAD_DIM:(h + 1) * HEAD_DIM] = acc_ref[c].astype(o_ref.dtype)

    if nq == 1:
        q_block(0)
    else:
        def q_step(qi, carry):
            q_block(qi)
            return carry
        lax.fori_loop(0, nq, q_step, 0)


def _attention(q, kn, vn, kc=None, vc=None, *, bq, bk, nq, heads):
    nb, t, d = q.shape
    hw = heads * HEAD_DIM
    has_cache = kc is not None
    assert has_cache or (bk == bq and bq % LANE == 0)
    assert not has_cache or (heads == N_HEADS and kc.shape[2] == HEAD_DIM and t == bq <= LANE and nq == 1)
    rh = min(bq, LANE)
    nchain = heads * (bq // rh)
    grid = (nb, d // hw, t // (bq * nq))
    qspec = pl.BlockSpec((1, bq * nq, hw), lambda b, g, i: (b, i, g))
    kvspec = lambda a: pl.BlockSpec((1, a.shape[1], hw), lambda b, g, i: (b, 0, g))
    cachespec = lambda a: pl.BlockSpec((1,) + a.shape[1:], lambda b, g, i: (b, 0, 0))
    u = jnp.tril(jnp.ones((bk, bk), F32), -1).astype(BF16)
    kv = [kn, vn] + ([kc, vc] if has_cache else [])
    scratch = [pltpu.VMEM((nchain, rh, HEAD_DIM), F32), pltpu.VMEM((nchain, rh, LANE), F32),
               pltpu.SMEM((1,), jnp.int32)]
    if has_cache:
        scratch += [pltpu.VMEM((LANE, hw), F32), pltpu.VMEM((LANE, hw), F32)]
    return pl.pallas_call(
        functools.partial(_attn_kernel, bq=bq, bk=bk, nq=nq, heads=heads, has_cache=has_cache),
        grid=grid,
        in_specs=[qspec, kvspec(kn), kvspec(vn)] + [cachespec(a) for a in kv[2:]]
        + [pl.BlockSpec(u.shape, lambda b, g, i: (0, 0))],
        out_specs=qspec,
        out_shape=jax.ShapeDtypeStruct((nb, t, d), BF16),
        scratch_shapes=scratch,
        compiler_params=pltpu.CompilerParams(dimension_semantics=("arbitrary", "arbitrary", "arbitrary"),
                                             vmem_limit_bytes=VMEM_LIMIT_BYTES),
        name="attn_cache" if has_cache else "attn",
    )(q, *kv, u)


def _back_kernel(x_ref, m_ref, sg_ref, ya_ref, fh_ref, wpa_ref, wo_ref, ln2_ref, wup_ref, fw_ref, fb_ref, wdn_ref,
                 y_ref, ft_ref, gp_ref, u2_ref, val_ref, *, bb, tm, d, dff, nsub, first):
    rows = bb * tm
    rs = rows // nsub
    kw = fw_ref.shape[0]
    pad = 8
    nchunk = dff // d
    subs = range(nsub)

    def load(ref, s):
        return ref[0, s * rs:(s + 1) * rs, :] if bb == 1 else ref[...].reshape(rows, d)

    def rsl(s):
        return slice(s * rs, (s + 1) * rs)

    @_when(first)
    def _():
        for c in range(nchunk):
            gp_ref[c, :, pad - (kw - 1):pad, :] = fh_ref[:, :, c * d:(c + 1) * d]

    atts = [jnp.dot(load(ya_ref, s), wpa_ref[...], preferred_element_type=F32) for s in subs]
    merged = [(load(m_ref, s).astype(F32) + load(sg_ref, s).astype(F32) * atts[s]).astype(BF16) for s in subs]
    outs = [jnp.dot(merged[s], wo_ref[...], preferred_element_type=F32) for s in subs]
    for s in subs:
        x1 = load(x_ref, s) + outs[s]
        if bb == 1:
            y_ref[0, rsl(s), :] = x1
        else:
            y_ref[...] = x1.reshape(bb, tm, d)
        u2_ref[rsl(s), :] = ((x1 * _rms_scale(x1)) * ln2_ref[...]).astype(BF16)

    def up(c):
        for s in subs:
            gp = jnp.dot(u2_ref[rsl(s), :], wup_ref[:, c * d:(c + 1) * d], preferred_element_type=F32)
            if bb == 1:
                gp_ref[c, 0, pad + s * rs:pad + (s + 1) * rs, :] = gp
            else:
                for b in range(bb):
                    gp_ref[c, b, pad:pad + tm, :] = gp[b * tm:(b + 1) * tm]
            val_ref[c % 2, rsl(s), :] = jnp.dot(u2_ref[rsl(s), :], wup_ref[:, dff + c * d:dff + (c + 1) * d],
                                                preferred_element_type=F32)

    def conv(c, b, start, n):
        cs = slice(c * d, (c + 1) * d)
        off = start - (kw - 1)
        g = gp_ref[c, b, off:off + n, :] * fw_ref[0:1, cs]
        for j in range(1, kw):
            g = g + gp_ref[c, b, off + j:off + j + n, :] * fw_ref[j:j + 1, cs]
        return g + fb_ref[:, cs]

    up(0)
    for c in range(nchunk):
        if c + 1 < nchunk:
            up(c + 1)
        cs = slice(c * d, (c + 1) * d)
        for s in subs:
            if bb == 1:
                gc = conv(c, 0, pad + s * rs, rs)
            else:
                gc = jnp.concatenate([conv(c, b, pad, tm) for b in range(bb)], axis=0)
            act = (_gelu_tanh(gc) * val_ref[c % 2, rsl(s), :]).astype(BF16)
            part = jnp.dot(act, wdn_ref[c * d:(c + 1) * d, :], preferred_element_type=F32)
            if bb == 1:
                y_ref[0, rsl(s), :] += part
            else:
                y_ref[...] += part.reshape(bb, tm, d)
        for b in range(bb):
            tail = gp_ref[c, b, pad + tm - (kw - 1):pad + tm, :]
            ft_ref[b, :, cs] = tail
            gp_ref[c, b, pad - (kw - 1):pad, :] = tail


def _back(prompt, sample, weights, *, tm, nsub):
    nb, t, d = prompt[0].shape
    bs, ts, _ = sample[0].shape
    dff = weights[-1].shape[0]
    kw = weights[4].shape[0]
    nt, nprompt, tile, stream, whole = _stream_specs(nb, t, tm, d)
    ins = tuple(prompt) + tuple(sample) + tuple(weights)
    big = (0, 1, 3, 6)

    def body(*refs):
        ins_p, ins_s, w = refs[0:5], refs[5:10], list(refs[10:10 + len(weights)])
        o = 10 + len(weights)
        outs_p, outs_s, scr_p, scr_s = refs[o:o + 2], refs[o + 2:o + 4], refs[o + 4:o + 7], refs[o + 7:o + 10]
        wbuf, sem_ref = refs[o + 10:o + 14], refs[o + 14]
        s = pl.program_id(0)

        @pl.when(s == 0)
        def _():
            slots = [v for c in range(dff // d) for v in _staging_slots(scr_p[0], (c, 0))]
            _load_as_bf16([(w[n], buf) for n, buf in zip(big, wbuf)], slots, sem_ref)
        for n, buf in zip(big, wbuf):
            w[n] = buf

        @pl.when(s < nprompt)
        def _():
            _back_kernel(*ins_p, *w, *outs_p, *scr_p, bb=1, tm=tm, d=d, dff=dff, nsub=nsub, first=s % nt == 0)

        @pl.when(s == nprompt)
        def _():
            _back_kernel(*ins_s, *w, *outs_s, *scr_s, bb=bs, tm=ts, d=d, dff=dff, nsub=1, first=True)

    out_shape = [jax.ShapeDtypeStruct((nb, t, d), F32), jax.ShapeDtypeStruct((nb, kw - 1, dff), F32),
                 jax.ShapeDtypeStruct((bs, ts, d), F32), jax.ShapeDtypeStruct((bs, kw - 1, dff), F32)]
    return pl.pallas_call(
        body,
        grid=(nprompt + 1,),
        in_specs=[tile] * 4 + [stream(kw - 1, dff)] + [whole(a.shape) for a in ins[5:10]]
        + [pl.BlockSpec(memory_space=pl.ANY) if n in big else whole(a.shape) for n, a in enumerate(weights)],
        out_specs=[tile, stream(kw - 1, dff)] + [whole(o.shape) for o in out_shape[2:]],
        out_shape=out_shape,
        scratch_shapes=[pltpu.VMEM((dff // d, 1, 8 + tm, d), F32), pltpu.VMEM((tm, d), BF16),
                        pltpu.VMEM((2, tm, d), F32),
                        pltpu.VMEM((dff // d, bs, 8 + ts, d), F32), pltpu.VMEM((bs * ts, d), BF16),
                        pltpu.VMEM((2, bs * ts, d), F32)]
        + [pltpu.VMEM(weights[n].shape, BF16) for n in big]
        + [pltpu.SemaphoreType.DMA((dff // d * (tm // STAGE_ROWS),))],
        compiler_params=pltpu.CompilerParams(dimension_semantics=("arbitrary",),
                                             vmem_limit_bytes=VMEM_LIMIT_BYTES),
        name="back",
    )(*ins)


def _dense_gate_blocks(w):
    nblk, c, _ = w.shape
    per = GATE_GROUP // c
    w4 = w.reshape(nblk // per, per, c, c)
    eye = jnp.eye(per, dtype=w.dtype)
    return jnp.einsum('gncd,nm->gncmd', w4, eye).reshape(nblk // per, GATE_GROUP, GATE_GROUP)


def kernel(x_prompt, x_sample, cache_k, cache_v, state_rnn_conv, state_rnn_h, state_ffn_conv, ln1, w_in, rnn_conv_w, rnn_conv_b, lru_wa, lru_ba, lru_wx, lru_bx, lru_lambda, q_norm_g, k_norm_g, w_proj_rnn, w_proj_attn, w_out, ln2, w_up, ffn_conv_w, ffn_conv_b, w_down):
    depth = ln1.shape[0]
    assert depth == 1
    bp, tp, d = x_prompt.shape
    bs, ts, _ = x_sample.shape
    dff = w_down.shape[1]
    row = lambda a: a[0].reshape(1, -1)
    wg = jnp.concatenate([_dense_gate_blocks(lru_wa[0]), _dense_gate_blocks(lru_wx[0])], axis=-1)
    front_w = (row(ln1), w_in[0], rnn_conv_w[0], row(rnn_conv_b), wg, row(lru_ba), row(lru_bx),
               row(lru_lambda), row(q_norm_g), row(k_norm_g), w_proj_rnn[0])
    back_w = (w_proj_attn[0], w_out[0], row(ln2), w_up[0],
              ffn_conv_w[0], row(ffn_conv_b), w_down[0])
    kw = rnn_conv_w.shape[1]
    fkw = ffn_conv_w.shape[1]

    zc = jnp.zeros((bp, kw - 1, d), F32)
    zh = jnp.zeros((bp, 1, d), F32)
    zf = jnp.zeros((bp, fkw - 1, dff), F32)
    (qp, kp, vp, mp, sgp, rcp, hp, qs, ks, vs, ms, sgs, rcs, hs) = _front(
        x_prompt, zc, zh, x_sample, state_rnn_conv[0], state_rnn_h[0].reshape(bs, 1, d), front_w, tm=512, nsub=2)
    yap = _attention(qp, kp, vp, bq=256, bk=256, nq=16, heads=4)
    yas = _attention(qs, ks, vs, cache_k[0].reshape(bs, -1, HEAD_DIM), cache_v[0].reshape(bs, -1, HEAD_DIM),
                     bq=ts, bk=256, nq=1, heads=N_HEADS)
    yp, fcp, ys, fcs = _back((x_prompt, mp, sgp, yap, zf), (x_sample, ms, sgs, yas, state_ffn_conv[0]), back_w,
                             tm=512, nsub=2)
    hd = lambda a: a.reshape(1, a.shape[0], a.shape[1], N_HEADS, HEAD_DIM)
    return (yp, ys, hd(kp), hd(vp), rcp[None], hp.reshape(1, bp, d), fcp[None],
            hd(ks), hd(vs), rcs[None], hs.reshape(1, bs, d), fcs[None])
```
